```python
import jax, jax.numpy as jnp
from jax import lax
import numpy as np

D_MODEL = 1024
BATCH = 2
SEQ = 8192
DEPTH = 4
DEC_BATCH = 128
DEC_SEQ = 4
PAST_LEN = 8192
PAGE_SIZE = 128

MIX_WIDTH = D_MODEL
HEAD_DIM = 64
RET_HEADS = 4
RET_V = MIX_WIDTH // 4 // RET_HEADS
RET_QK = RET_V // 2
RET_THETA = 10000.0
HG_HEADS = 4
HG_V = MIX_WIDTH // 4 // HG_HEADS
HG_K = HG_V
SWA_HD = HEAD_DIM
SWA_HEADS = (MIX_WIDTH // 2) // SWA_HD
SWA_KV = 2
SWA_GROUP = SWA_HEADS // SWA_KV
WINDOW = 128
SWA_BUF = WINDOW
ROT_DIM = SWA_HD // 4
ROPE_THETA = 500000.0
CHUNK = 128
D_FF = 4 * D_MODEL
ALPHA = (2.0 * DEPTH) ** 0.25
BETA = (8.0 * DEPTH) ** -0.25
LN_EPS = 1e-5
RMS_EPS = 1e-6

IN_SPLITS = (RET_HEADS * RET_QK, RET_HEADS * RET_QK, RET_HEADS * RET_V, RET_HEADS * RET_V,
             HG_HEADS * HG_K, HG_HEADS * HG_K, HG_HEADS * HG_V, HG_HEADS * HG_V,
             SWA_HEADS * SWA_HD, SWA_KV * SWA_HD, SWA_KV * SWA_HD)
HEAD_SHAPES = ((RET_HEADS, RET_QK), (RET_HEADS, RET_QK), (RET_HEADS, RET_V), (RET_HEADS, RET_V),
               (HG_HEADS, HG_K), (HG_HEADS, HG_K), (HG_HEADS, HG_V), (HG_HEADS, HG_V),
               (SWA_HEADS, SWA_HD), (SWA_KV, SWA_HD), (SWA_KV, SWA_HD))
V_PARTS = (2, 6, 10)
IN_COLS = sum(IN_SPLITS)

kernel_name = 'hymba_retention_hgrn2_swa_deepnorm_step'


def _layernorm(x, g, b):
    xf = x.astype(jnp.float32)
    mu = jnp.mean(xf, -1, keepdims=True)
    var = jnp.mean(jnp.square(xf - mu), -1, keepdims=True)
    return ((xf - mu) * lax.rsqrt(var + LN_EPS) * g.astype(jnp.float32) + b.astype(jnp.float32)).astype(x.dtype)


def _rmsnorm(x):
    xf = x.astype(jnp.float32)
    return xf * lax.rsqrt(jnp.mean(jnp.square(xf), -1, keepdims=True) + RMS_EPS)


def _rope(x, pos, inv_freq):
    r = 2 * inv_freq.shape[0]
    ang = pos.astype(jnp.float32)[:, None] * inv_freq[None, :]
    cos = jnp.cos(ang)[:, None, :]
    sin = jnp.sin(ang)[:, None, :]
    xr = x[..., :r].astype(jnp.float32)
    x1, x2 = xr[..., :r // 2], xr[..., r // 2:]
    rot = jnp.concatenate([x1 * cos - x2 * sin, x2 * cos + x1 * sin], -1).astype(x.dtype)
    return jnp.concatenate([rot, x[..., r:]], -1)


def _project(x, w_in):
    n, l = x.shape[:2]
    h = jnp.einsum('nld,dc->nlc', x, w_in)
    offs = [int(o) for o in np.cumsum(IN_SPLITS)[:-1]]
    parts = jnp.split(h, offs, axis=-1)
    return [p.reshape(n, l, *hs) for p, hs in zip(parts, HEAD_SHAPES)]


def _ret_log_decay():
    return jnp.log(1.0 - 2.0 ** (-5.0 - jnp.arange(RET_HEADS, dtype=jnp.float32)))


def _ret_qk(rq, rk, pos):
    inv = RET_THETA ** (-jnp.linspace(0.0, 1.0, RET_QK // 2, dtype=jnp.float32))
    return _rope(rq, pos, inv), _rope(rk, pos, inv) * (RET_QK ** -0.5)


def _swa_rope(x, pos):
    inv = ROPE_THETA ** (-jnp.arange(0, ROT_DIM, 2, dtype=jnp.float32) / ROT_DIM)
    return _rope(x, pos, inv)


def _hg_gates(hf, lb):
    lb = lb.reshape(HG_HEADS, HG_K)
    logf = jnp.logaddexp(jnp.log(lb), jnp.log1p(-lb) + jax.nn.log_sigmoid(hf.astype(jnp.float32)))
    return logf, -jnp.expm1(logf)


def _ret_chunk(q, k, v, s0):
    q, k, v, s0 = (a.astype(jnp.float32) for a in (q, k, v, s0))
    L = q.shape[1]
    lg = _ret_log_decay()
    idx = jnp.arange(L, dtype=jnp.float32)
    diff = idx[:, None] - idx[None, :]
    dec = jnp.where(diff >= 0, jnp.exp(jnp.maximum(diff, 0.0)[None] * lg[:, None, None]), 0.0)
    a = jnp.einsum('nthk,nshk->nhts', q, k) * dec[None]
    inter = jnp.exp((idx + 1.0)[:, None] * lg[None, :])[None, :, :, None]
    o = jnp.einsum('nhts,nshv->nthv', a, v) + jnp.einsum('nthk,nhkv->nthv', q, s0) * inter
    kdec = k * jnp.exp((L - 1.0 - idx)[:, None] * lg[None, :])[None, :, :, None]
    s = jnp.exp(L * lg)[None, :, None, None] * s0 + jnp.einsum('nshk,nshv->nhkv', kdec, v)
    return o, s


def _hg_chunk(q, k, v, logf, s0):
    q, k, v, s0 = (a.astype(jnp.float32) for a in (q, k, v, s0))
    L = q.shape[1]
    b = jnp.cumsum(logf, axis=1)
    causal = jnp.tril(jnp.ones((L, L), dtype=bool))
    rel = b[:, :, None] - b[:, None, :]
    dec = jnp.where(causal[None, :, :, None, None], jnp.exp(jnp.minimum(rel, 0.0)), 0.0)
    a = jnp.einsum('nthk,nshk,ntshk->nhts', q, k, dec)
    o = jnp.einsum('nhts,nshv->nthv', a, v) + jnp.einsum('nthk,nhkv->nthv', q * jnp.exp(b), s0)
    bl = b[:, -1]
    s = jnp.exp(bl)[..., None] * s0 + jnp.einsum('nshk,nshv->nhkv', k * jnp.exp(bl[:, None] - b), v)
    return o, s


def _scan_chunks(chunk_fn, arrays, s0):
    n, S = arrays[0].shape[:2]
    nc = S // CHUNK
    xs = tuple(jnp.moveaxis(a.reshape(n, nc, CHUNK, *a.shape[2:]), 1, 0) for a in arrays)

    def step(s, c):
        o, s = chunk_fn(*c, s)
        return s, o

    s, o = lax.scan(step, s0, xs)
    o = jnp.moveaxis(o, 0, 1)
    return o.reshape(n, S, *o.shape[3:]), s


def _sink_attn(q, k, v, mask, sinks):
    s = jnp.einsum('nmqkgd,nmskd->nmkgqs', q, k).astype(jnp.float32) * (SWA_HD ** -0.5)
    s = jnp.where(mask[None, :, None, None], s, -jnp.inf)
    sink = sinks.astype(jnp.float32).reshape(SWA_KV, SWA_GROUP)[None, None, :, :, None, None]
    m = jnp.maximum(jnp.max(s, -1, keepdims=True), sink)
    p = jnp.exp(s - m)
    den = jnp.sum(p, -1, keepdims=True) + jnp.exp(sink - m)
    o = jnp.einsum('nmkgqs,nmskd->nmqkgd', p / den, v.astype(jnp.float32))
    n, mb, lq = o.shape[:3]
    return o.reshape(n, mb * lq, SWA_HEADS * SWA_HD)


def _swa_prompt(q, k, v, sinks):
    n, S = q.shape[:2]
    nb = S // WINDOW
    qb = q.reshape(n, nb, WINDOW, SWA_KV, SWA_GROUP, SWA_HD)
    kb = k.reshape(n, nb, WINDOW, SWA_KV, SWA_HD)
    vb = v.reshape(n, nb, WINDOW, SWA_KV, SWA_HD)
    pad = ((0, 0), (1, 0), (0, 0), (0, 0), (0, 0))
    kk = jnp.concatenate([jnp.pad(kb, pad)[:, :-1], kb], axis=2)
    vv = jnp.concatenate([jnp.pad(vb, pad)[:, :-1], vb], axis=2)
    blk = jnp.arange(nb)[:, None]
    qpos = blk * WINDOW + jnp.arange(WINDOW)[None, :]
    kpos = (blk - 1) * WINDOW + jnp.arange(2 * WINDOW)[None, :]
    d = qpos[:, :, None] - kpos[:, None, :]
    mask = (d >= 0) & (d <= WINDOW) & (kpos[:, None, :] >= 0)
    return _sink_attn(qb, kk, vv, mask, sinks)


def _swa_sample(q, k, v, ck, cv, sinks):
    n, L = q.shape[:2]
    kk = jnp.concatenate([ck.astype(k.dtype), k], axis=1)
    vv = jnp.concatenate([cv.astype(v.dtype), v], axis=1)
    qpos = PAST_LEN + jnp.arange(L)
    kpos = PAST_LEN - SWA_BUF + jnp.arange(SWA_BUF + L)
    d = qpos[:, None] - kpos[None, :]
    mask = ((d >= 0) & (d <= WINDOW))[None]
    o = _sink_attn(q.reshape(n, 1, L, SWA_KV, SWA_GROUP, SWA_HD), kk[:, None], vv[:, None], mask, sinks)
    return o, kk[:, -SWA_BUF:], vv[:, -SWA_BUF:]


def _merge(o_r, g_r, o_h, g_h, o_s, hg_norm_w, w_out, dtype):
    n, l = o_r.shape[:2]
    ret = _rmsnorm(o_r) * jax.nn.silu(g_r.astype(jnp.float32))
    hg = _rmsnorm(o_h) * hg_norm_w.astype(jnp.float32) * jax.nn.silu(g_h.astype(jnp.float32))
    cat = jnp.concatenate([ret.reshape(n, l, -1), hg.reshape(n, l, -1),
                           o_s.astype(jnp.float32)], -1).astype(dtype)
    return jnp.einsum('nlc,cd->nld', cat, w_out)


def _mix_prompt(x, w_in, w_out, sinks, lb, hg_norm_w):
    n, S = x.shape[:2]
    pos = jnp.arange(S)
    rq, rk, rv, rg, hq, hf, hi, hgt, sq, sk, sv = _project(x, w_in)
    q_r, k_r = _ret_qk(rq, rk, pos)
    o_r, st_r = _scan_chunks(_ret_chunk, (q_r, k_r, rv), jnp.zeros((n, RET_HEADS, RET_QK, RET_V), jnp.float32))
    logf, k_h = _hg_gates(hf, lb)
    o_h, st_h = _scan_chunks(_hg_chunk, (hq, k_h, hi, logf), jnp.zeros((n, HG_HEADS, HG_K, HG_V), jnp.float32))
    q_s, k_s = _swa_rope(sq, pos), _swa_rope(sk, pos)
    o_s = _swa_prompt(q_s, k_s, sv, sinks)
    m = _merge(o_r, rg, o_h, hgt, o_s, hg_norm_w, w_out, x.dtype)
    return m, st_r, st_h, k_s[:, -SWA_BUF:], sv[:, -SWA_BUF:]


def _mix_sample(x, s_r, s_h, ck, cv, w_in, w_out, sinks, lb, hg_norm_w):
    L = x.shape[1]
    pos = PAST_LEN + jnp.arange(L)
    rq, rk, rv, rg, hq, hf, hi, hgt, sq, sk, sv = _project(x, w_in)
    q_r, k_r = _ret_qk(rq, rk, pos)
    o_r, st_r = _ret_chunk(q_r, k_r, rv, s_r)
    logf, k_h = _hg_gates(hf, lb)
    o_h, st_h = _hg_chunk(hq, k_h, hi, logf, s_h)
    q_s, k_s = _swa_rope(sq, pos), _swa_rope(sk, pos)
    o_s, nk, nv = _swa_sample(q_s, k_s, sv, ck, cv, sinks)
    m = _merge(o_r, rg, o_h, hgt, o_s, hg_norm_w, w_out, x.dtype)
    return m, st_r, st_h, nk, nv


def _post_block(x, mix, g1, b1, w_up, w_down, g2, b2):
    x = _layernorm(ALPHA * x + mix, g1, b1)
    h = jax.nn.relu(jnp.einsum('nld,df->nlf', x, w_up))
    return _layernorm(ALPHA * x + jnp.einsum('nlf,fd->nld', h * h, w_down), g2, b2)


def setup_inputs(seed: int = 0) -> dict:
    key = jax.random.key(seed)
    ks = jax.random.split(key, 20)
    nrm = jax.random.normal
    col_scale = jnp.concatenate([jnp.full((c,), BETA if i in V_PARTS else 1.0, jnp.float32)
                                 for i, c in enumerate(IN_SPLITS)])
    return {
        'x_prompt': nrm(ks[0], (BATCH, SEQ, D_MODEL), jnp.float32),
        'x_sample': nrm(ks[1], (DEC_BATCH, DEC_SEQ, D_MODEL), jnp.float32),
        'state_ret': 0.5 * nrm(ks[2], (DEPTH, DEC_BATCH, RET_HEADS, RET_QK, RET_V), jnp.float32),
        'state_hgrn': 0.5 * nrm(ks[3], (DEPTH, DEC_BATCH, HG_HEADS, HG_K, HG_V), jnp.float32),
        'cache_swa_k': nrm(ks[4], (DEPTH, DEC_BATCH, SWA_BUF, SWA_KV, SWA_HD), jnp.float32),
        'cache_swa_v': nrm(ks[5], (DEPTH, DEC_BATCH, SWA_BUF, SWA_KV, SWA_HD), jnp.float32),
        'w_in': nrm(ks[6], (DEPTH, D_MODEL, IN_COLS), jnp.float32) * (D_MODEL ** -0.5) * col_scale,
        'w_out': nrm(ks[7], (DEPTH, MIX_WIDTH, D_MODEL), jnp.float32) * (MIX_WIDTH ** -0.5) * BETA,
        'swa_sinks': 0.5 * nrm(ks[8], (DEPTH, SWA_HEADS), jnp.float32),
        'hg_lb_raw': 0.1 * nrm(ks[9], (DEPTH, HG_HEADS * HG_K), jnp.float32),
        'hg_norm_w': 1.0 + 0.1 * nrm(ks[10], (DEPTH, HG_V), jnp.float32),
        'ln_mix_g': 1.0 + 0.1 * nrm(ks[11], (DEPTH, D_MODEL), jnp.float32),
        'ln_mix_b': 0.1 * nrm(ks[12], (DEPTH, D_MODEL), jnp.float32),
        'w_up': nrm(ks[13], (DEPTH, D_MODEL, D_FF), jnp.float32) * (D_MODEL ** -0.5),
        'w_down': nrm(ks[14], (DEPTH, D_FF, D_MODEL), jnp.float32) * (D_FF ** -0.5) * BETA,
        'ln_ffn_g': 1.0 + 0.1 * nrm(ks[15], (DEPTH, D_MODEL), jnp.float32),
        'ln_ffn_b': 0.1 * nrm(ks[16], (DEPTH, D_MODEL), jnp.float32),
    }


def reference(x_prompt, x_sample, state_ret, state_hgrn, cache_swa_k, cache_swa_v,
              w_in, w_out, swa_sinks, hg_lb_raw, hg_norm_w, ln_mix_g, ln_mix_b,
              w_up, w_down, ln_ffn_g, ln_ffn_b):
    lb_all = jnp.cumsum(jax.nn.softmax(hg_lb_raw.astype(jnp.float32), axis=0), axis=0)
    lb = lb_all - lb_all[0:1]
    xp, xs = x_prompt, x_sample
    p_r, p_h, p_k, p_v = [], [], [], []
    s_r, s_h, s_k, s_v = [], [], [], []
    for l in range(DEPTH):
        m, a, b, c, d = _mix_prompt(xp, w_in[l], w_out[l], swa_sinks[l], lb[l], hg_norm_w[l])
        xp = _post_block(xp, m, ln_mix_g[l], ln_mix_b[l], w_up[l], w_down[l], ln_ffn_g[l], ln_ffn_b[l])
        p_r.append(a); p_h.append(b); p_k.append(c); p_v.append(d)
        m, a, b, c, d = _mix_sample(xs, state_ret[l], state_hgrn[l], cache_swa_k[l], cache_swa_v[l],
                                    w_in[l], w_out[l], swa_sinks[l], lb[l], hg_norm_w[l])
        xs = _post_block(xs, m, ln_mix_g[l], ln_mix_b[l], w_up[l], w_down[l], ln_ffn_g[l], ln_ffn_b[l])
        s_r.append(a); s_h.append(b); s_k.append(c); s_v.append(d)
    return (xp, xs, jnp.stack(p_r), jnp.stack(p_h), jnp.stack(p_k), jnp.stack(p_v),
            jnp.stack(s_r), jnp.stack(s_h), jnp.stack(s_k), jnp.stack(s_v))
```

```python
import functools

import numpy as np
import jax
import jax.numpy as jnp
from jax import lax
from jax.experimental import pallas as pl
from jax.experimental.pallas import tpu as pltpu

F32 = jnp.float32
BF16 = jnp.bfloat16

D_MODEL = 1024
BATCH = 2
SEQ = 8192
DEPTH = 4
DEC_BATCH = 128
DEC_SEQ = 4
PAST_LEN = 8192
RET_HEADS = 4
RET_QK = 32
RET_V = 64
RET_THETA = 10000.0
HG_HEADS = 4
HG_K = 64
HG_V = 64
SWA_HEADS = 8
SWA_KV = 2
SWA_GROUP = SWA_HEADS // SWA_KV
SWA_HD = 64
WINDOW = 128
ROT_DIM = 16
ROPE_THETA = 500000.0
CHUNK = 128
D_FF = 4 * D_MODEL
IN_COLS = 2560
ALPHA = (2.0 * DEPTH) ** 0.25
LN_EPS = 1e-5
RMS_EPS = 1e-6
NEG = -1e30

C_RQ, C_RK, C_RV, C_RG = 0, 128, 256, 512
C_HQ, C_HF, C_HI, C_HG = 768, 1024, 1280, 1536
C_SQ, C_SK, C_SV = 1792, 2304, 2432

LANES = 128
PROMPT_TILE = 512
FFN_TILE = 512
SAMPLE_SEQS = 16
SAMPLE_ROWS = SAMPLE_SEQS * DEC_SEQ
PROMPT_LEVELS = 7
SAMPLE_LEVELS = 2
VMEM_LIMIT = 56 * 1024 * 1024


def _mm(a, b):
    return jnp.dot(a.astype(BF16), b.astype(BF16), preferred_element_type=F32)


def _mm_nt(a, b):
    return lax.dot_general(a.astype(BF16), b.astype(BF16), (((1,), (1,)), ((), ())),
                           preferred_element_type=F32)


def _iota(shape, dim):
    return lax.broadcasted_iota(jnp.int32, shape, dim)


def _layernorm(x, g, b):
    mu = jnp.mean(x, axis=-1, keepdims=True)
    xc = x - mu
    var = jnp.mean(xc * xc, axis=-1, keepdims=True)
    return xc * lax.rsqrt(var + LN_EPS) * g + b


def _rope(x, cos_t, sin_t, half, period):
    lane = _iota(x.shape, 1) % period
    fwd = pltpu.roll(x, LANES - half, 1)
    bwd = pltpu.roll(x, half, 1)
    return x * cos_t + jnp.where(lane < half, fwd, bwd) * sin_t


def _silu(x):
    return x * (1.0 / (1.0 + jnp.exp(-x)))


def _split3(x):
    hi = x.astype(BF16)
    r = x - hi.astype(F32)
    mid = r.astype(BF16)
    lo = (r - mid.astype(F32)).astype(BF16)
    return hi, mid, lo


def _hg_lower_bound(lbraw_ref, layer):
    raw = lbraw_ref[...]
    e = jnp.exp(raw - jnp.max(raw, axis=0, keepdims=True))
    sm = e / jnp.sum(e, axis=0, keepdims=True)
    cum = sm[0:1]
    for i in range(1, layer + 1):
        cum = cum + sm[i:i + 1]
    return cum - sm[0:1]


def _hg_log_forget(hf, lb):
    logsig = jnp.minimum(hf, 0.0) - jnp.log1p(jnp.exp(-jnp.abs(hf)))
    a = jnp.log(lb)
    b = jnp.log1p(-lb) + logsig
    return jnp.maximum(a, b) + jnp.log1p(jnp.exp(-jnp.abs(a - b)))


def _head_blockdiag(v, rows):
    r = _iota((4 * rows, 256), 0) // rows
    c = _iota((4 * rows, 256), 1) // 64
    return jnp.where(r == c, jnp.tile(v, (4, 1)), 0.0).astype(BF16)


def _rms_gate(o, gate, hblk_ref, w=None):
    ms = _mm(o * o, hblk_ref[...]) * (1.0 / 64.0)
    y = o * lax.rsqrt(ms + RMS_EPS)
    if w is not None:
        y = y * w
    return y * _silu(gate)


def _ret_intra(q, k, v, dec_ref, rows):
    lane_head = _iota((rows, 128), 1) // RET_QK
    a = [_mm_nt(jnp.where(lane_head == h, q, 0.0), k) * dec_ref[h] for h in range(RET_HEADS)]
    return _mm(jnp.concatenate(a, axis=1), _head_blockdiag(v, rows))


def _hgrn_intra(q, k, v, lf, mall_ref, hblk_ref, rows, levels):
    l3 = jnp.concatenate(_split3(lf), axis=1)
    eb = jnp.dot(mall_ref[...], l3, preferred_element_type=F32)
    eb = eb[:, 0:256] + eb[:, 256:512] + eb[:, 512:768]
    b = eb[0:rows]
    tail = eb[(levels + 1) * rows:(levels + 2) * rows]
    row = _iota((rows, 256), 0)
    lane_head = _iota((rows, 256), 1) // HG_K
    ra = _iota((rows, rows), 0)
    ca = _iota((rows, rows), 1)
    acc = [jnp.zeros((rows, rows), F32) for _ in range(HG_HEADS)]
    for l in range(levels):
        w = jnp.exp(eb[(l + 1) * rows:(l + 2) * rows])
        lower = ((row >> l) & 1) == 1
        qs = jnp.where(lower, q * w, 0.0).astype(BF16)
        ks = jnp.where(lower, 0.0, k * w).astype(BF16)
        same_block = (ra >> (l + 1)) == (ca >> (l + 1))
        for h in range(HG_HEADS):
            p = _mm_nt(jnp.where(lane_head == h, qs, jnp.zeros_like(qs)), ks)
            acc[h] = acc[h] + jnp.where(same_block, p, 0.0)
    o = _mm(jnp.concatenate(acc, axis=1), _head_blockdiag(v, rows))
    o = o + _mm(q * k, hblk_ref[...]) * v
    return o, b, tail


def _swa_heads(qcols, rope_q):
    out = []
    for c in range(4):
        qc = rope_q(qcols[:, LANES * c:LANES * (c + 1)]) * (SWA_HD ** -0.5)
        half = _iota(qc.shape, 1) // SWA_HD
        for p in range(2):
            g = (2 * c + p) // SWA_GROUP
            qm = jnp.where(half == p, qc, 0.0)
            if p != g:
                qm = pltpu.roll(qm, SWA_HD, 1)
            out.append(qm)
    return out


def _prompt_mix_kernel(sinks_ref, x_ref, win_ref, wout_ref, lbraw_ref, hgw_ref, g1_ref, b1_ref,
                       cr_ref, sr_ref, cs_ref, ss_ref, dec_ref, inter_ref, kdec_ref, rowdec_ref,
                       mall_ref, hblk_ref,
                       x1_ref, str_ref, sth_ref, pk_ref, pv_ref,
                       h_scr, cat_scr, sret, shg, kprev, vprev, *, layer):
    j = pl.program_id(1)

    @pl.when(j == 0)
    def _():
        sret[...] = jnp.zeros_like(sret)
        shg[...] = jnp.zeros_like(shg)
        kprev[...] = jnp.zeros_like(kprev)
        vprev[...] = jnp.zeros_like(vprev)

    h_scr[...] = jnp.dot(x_ref[...].astype(BF16), win_ref[...], preferred_element_type=F32)
    lb = _hg_lower_bound(lbraw_ref, layer)
    n_chunks = PROMPT_TILE // CHUNK

    def chunk(c, carry):
        r0 = pl.multiple_of(c * CHUNK, CHUNK)
        rows = pl.ds(r0, CHUNK)
        cr, sr = cr_ref[rows, :], sr_ref[rows, :]
        cs, ss = cs_ref[rows, :], ss_ref[rows, :]

        q = _rope(h_scr[rows, C_RQ:C_RQ + 128], cr, sr, RET_QK // 2, RET_QK)
        k = _rope(h_scr[rows, C_RK:C_RK + 128], cr, sr, RET_QK // 2, RET_QK) * (RET_QK ** -0.5)
        v = h_scr[rows, C_RV:C_RV + 256]
        o_r = _ret_intra(q, k, v, dec_ref, CHUNK)
        s0 = sret[...]
        o_r = o_r + _mm(q, s0) * inter_ref[...]
        r_i = _iota((128, 256), 0) // RET_QK
        c_i = _iota((128, 256), 1) // RET_V
        upd = _mm((k * kdec_ref[...]).T, v)
        sret[...] = rowdec_ref[...] * s0 + jnp.where(r_i == c_i, upd, 0.0)
        ret = _rms_gate(o_r, h_scr[rows, C_RG:C_RG + 256], hblk_ref)

        q = h_scr[rows, C_HQ:C_HQ + 256]
        lf = _hg_log_forget(h_scr[rows, C_HF:C_HF + 256], lb)
        k = 1.0 - jnp.exp(lf)
        v = h_scr[rows, C_HI:C_HI + 256]
        o_h, b, tail = _hgrn_intra(q, k, v, lf, mall_ref, hblk_ref, CHUNK, PROMPT_LEVELS)
        s0 = shg[...]
        o_h = o_h + _mm(q * jnp.exp(b), s0)
        decay_t = jnp.broadcast_to(jnp.exp(b[CHUNK - 1:CHUNK, :]), (128, 256)).T
        r_i = _iota((256, 256), 0) // HG_K
        c_i = _iota((256, 256), 1) // HG_V
        upd = _mm((k * jnp.exp(tail)).T, v)
        shg[...] = jnp.tile(decay_t, (1, 2)) * s0 + jnp.where(r_i == c_i, upd, 0.0)
        hg = _rms_gate(o_h, h_scr[rows, C_HG:C_HG + 256], hblk_ref, hgw_ref[...])

        k = _rope(h_scr[rows, C_SK:C_SK + 128], cs, ss, ROT_DIM // 2, SWA_HD)
        v = h_scr[rows, C_SV:C_SV + 128]
        kk = jnp.concatenate([kprev[...], k.astype(BF16)], axis=0)
        vv = jnp.concatenate([vprev[...], v.astype(BF16)], axis=0)
        qi = _iota((128, 256), 0)
        kj = _iota((128, 256), 1)
        first = jnp.where(j * n_chunks + c > 0, 0, WINDOW)
        mask = (kj >= qi) & (kj <= qi + WINDOW) & (kj >= first)
        qm = _swa_heads(h_scr[rows, C_SQ:C_SQ + 512],
                        lambda t: _rope(t, cs, ss, ROT_DIM // 2, SWA_HD))
        outs = []
        for hd in range(SWA_HEADS):
            g = hd // SWA_GROUP
            sink = sinks_ref[layer, hd]
            s = jnp.where(mask, _mm_nt(qm[hd], kk), NEG)
            m = jnp.maximum(jnp.max(s, axis=1, keepdims=True), sink)
            p = jnp.exp(s - m)
            den = jnp.sum(p, axis=1, keepdims=True) + jnp.exp(sink - m)
            outs.append(_mm(p, vv)[:, SWA_HD * g:SWA_HD * (g + 1)] / den)
        kprev[...] = k.astype(BF16)
        vprev[...] = v.astype(BF16)
        pk_ref[...] = k
        pv_ref[...] = v

        cat_scr[rows, :] = jnp.concatenate([ret, hg] + outs, axis=1).astype(BF16)
        return carry

    lax.fori_loop(0, n_chunks, chunk, 0)
    mix = jnp.dot(cat_scr[...], wout_ref[...], preferred_element_type=F32)
    x1_ref[...] = _layernorm(ALPHA * x_ref[...] + mix, g1_ref[...], b1_ref[...])
    str_ref[...] = sret[...]
    sth_ref[...] = shg[...]


def _const_spec(shape, ngrid):
    zeros = (0,) * len(shape)
    if ngrid == 1:
        return pl.BlockSpec(shape, lambda i: zeros)
    return pl.BlockSpec(shape, lambda b, j: zeros)


def _prompt_mix(layer, x, w_in, w_out, sinks, lbraw, hgw, g1, b1, tabs, consts):
    t = PROMPT_TILE
    wspec = lambda r, c: pl.BlockSpec((None, r, c), lambda b, j: (layer, 0, 0),
                                      pipeline_mode=pl.Buffered(1))
    vspec = lambda c: pl.BlockSpec((None, 1, c), lambda b, j: (layer, 0, 0))
    tspec = pl.BlockSpec((t, LANES), lambda b, j: (j, 0))
    in_specs = [pl.BlockSpec(memory_space=pltpu.SMEM),
                pl.BlockSpec((None, t, D_MODEL), lambda b, j: (b, j, 0)),
                wspec(D_MODEL, IN_COLS), wspec(D_MODEL, D_MODEL),
                _const_spec((DEPTH, 256), 2), vspec(256), vspec(D_MODEL), vspec(D_MODEL),
                tspec, tspec, tspec, tspec] + [_const_spec(c.shape, 2) for c in consts]
    out_shape = [jax.ShapeDtypeStruct((BATCH, SEQ, D_MODEL), F32),
                 jax.ShapeDtypeStruct((BATCH, 128, 256), F32),
                 jax.ShapeDtypeStruct((BATCH, 256, 256), F32),
                 jax.ShapeDtypeStruct((BATCH, WINDOW, 128), F32),
                 jax.ShapeDtypeStruct((BATCH, WINDOW, 128), F32)]
    out_specs = [pl.BlockSpec((None, t, D_MODEL), lambda b, j: (b, j, 0)),
                 pl.BlockSpec((None, 128, 256), lambda b, j: (b, 0, 0)),
                 pl.BlockSpec((None, 256, 256), lambda b, j: (b, 0, 0)),
                 pl.BlockSpec((None, WINDOW, 128), lambda b, j: (b, 0, 0)),
                 pl.BlockSpec((None, WINDOW, 128), lambda b, j: (b, 0, 0))]
    scratch = [pltpu.VMEM((t, IN_COLS), F32), pltpu.VMEM((t, D_MODEL), BF16),
               pltpu.VMEM((128, 256), F32), pltpu.VMEM((256, 256), F32),
               pltpu.VMEM((WINDOW, 128), BF16), pltpu.VMEM((WINDOW, 128), BF16)]
    return pl.pallas_call(
        functools.partial(_prompt_mix_kernel, layer=layer),
        grid=(BATCH, SEQ // t), in_specs=in_specs, out_specs=out_specs, out_shape=out_shape,
        scratch_shapes=scratch, name=f"prompt_mix_{layer}",
        compiler_params=pltpu.CompilerParams(dimension_semantics=("arbitrary", "arbitrary"),
                                             vmem_limit_bytes=VMEM_LIMIT),
    )(sinks, x, w_in, w_out, lbraw, hgw, g1, b1, *tabs, *consts)


def _ffn_kernel(x_ref, wup_ref, wdn_ref, g_ref, b_ref, o_ref):
    x = x_ref[...]
    xb = x.astype(BF16)
    acc = jnp.zeros(x.shape, F32)
    step = 1024
    for i in range(D_FF // step):
        hid = jnp.maximum(jnp.dot(xb, wup_ref[:, i * step:(i + 1) * step],
                                  preferred_element_type=F32), 0.0)
        acc = acc + jnp.dot((hid * hid).astype(BF16), wdn_ref[i * step:(i + 1) * step, :],
                            preferred_element_type=F32)
    o_ref[...] = _layernorm(ALPHA * x + acc, g_ref[...], b_ref[...])


def _ffn(layer, x, w_up, w_down, g2, b2):
    rows = x.shape[0]
    t = FFN_TILE
    return pl.pallas_call(
        _ffn_kernel,
        grid=(rows // t,),
        in_specs=[pl.BlockSpec((t, D_MODEL), lambda i: (i, 0)),
                  pl.BlockSpec((None, D_MODEL, D_FF), lambda i: (layer, 0, 0),
                               pipeline_mode=pl.Buffered(1)),
                  pl.BlockSpec((None, D_FF, D_MODEL), lambda i: (layer, 0, 0),
                               pipeline_mode=pl.Buffered(1)),
                  pl.BlockSpec((None, 1, D_MODEL), lambda i: (layer, 0, 0)),
                  pl.BlockSpec((None, 1, D_MODEL), lambda i: (layer, 0, 0))],
        out_specs=pl.BlockSpec((t, D_MODEL), lambda i: (i, 0)),
        out_shape=jax.ShapeDtypeStruct((rows, D_MODEL), F32),
        name=f"ffn_{layer}_{rows}",
        compiler_params=pltpu.CompilerParams(dimension_semantics=("arbitrary",),
                                             vmem_limit_bytes=VMEM_LIMIT),
    )(x, w_up, w_down, g2, b2)


def _sample_mix_kernel(sinks_ref, x_ref, win_ref, wout_ref, lbraw_ref, hgw_ref, g1_ref, b1_ref,
                       cr_ref, sr_ref, cs_ref, ss_ref, sret_ref, shg_ref, ck_ref, cv_ref,
                       dec_ref, inter_ref, kdec_ref, mall_ref, hblk_ref,
                       x1_ref, nsret_ref, nshg_ref, nk_ref, nv_ref, os_scr, *, layer, gam4):
    rows, seqs = SAMPLE_ROWS, SAMPLE_SEQS
    x = x_ref[...]
    h = jnp.dot(x.astype(BF16), win_ref[...], preferred_element_type=F32)
    cr, sr, cs, ss = cr_ref[...], sr_ref[...], cs_ref[...], ss_ref[...]
    lb = _hg_lower_bound(lbraw_ref, layer)

    q = _rope(h[:, C_RQ:C_RQ + 128], cr, sr, RET_QK // 2, RET_QK)
    k = _rope(h[:, C_RK:C_RK + 128], cr, sr, RET_QK // 2, RET_QK) * (RET_QK ** -0.5)
    v = h[:, C_RV:C_RV + 256]
    o_r = _ret_intra(q, k, v, dec_ref, rows)
    hk = RET_HEADS * RET_QK
    s0 = sret_ref[...].reshape(seqs * hk, RET_V)
    r_i = (_iota((seqs * hk, 256), 0) % hk) // RET_QK
    c_i = _iota((seqs * hk, 256), 1) // RET_V
    s0bd = jnp.where(r_i == c_i, jnp.tile(s0, (1, 4)), 0.0).astype(BF16)
    own = _iota((rows, seqs * hk), 0) // DEC_SEQ == _iota((rows, seqs * hk), 1) // hk
    o_r = o_r + _mm(jnp.where(own, jnp.tile(q, (1, seqs)), 0.0), s0bd) * inter_ref[...]
    kdt = (k * kdec_ref[...]).T
    col_seq = _iota((hk, rows), 1) // DEC_SEQ
    for n in range(seqs):
        upd = _mm(jnp.where(col_seq == n, kdt, 0.0), v)
        for hd in range(RET_HEADS):
            nsret_ref[n, hd] = (gam4[hd] * sret_ref[n, hd]
                                + upd[RET_QK * hd:RET_QK * (hd + 1), RET_V * hd:RET_V * (hd + 1)])
    ret = _rms_gate(o_r, h[:, C_RG:C_RG + 256], hblk_ref)

    q = h[:, C_HQ:C_HQ + 256]
    lf = _hg_log_forget(h[:, C_HF:C_HF + 256], lb)
    k = 1.0 - jnp.exp(lf)
    v = h[:, C_HI:C_HI + 256]
    o_h, b, tail = _hgrn_intra(q, k, v, lf, mall_ref, hblk_ref, rows, SAMPLE_LEVELS)
    hk = HG_HEADS * HG_K
    s0 = shg_ref[...].reshape(seqs * hk, HG_V)
    r_i = (_iota((seqs * hk, 256), 0) % hk) // HG_K
    c_i = _iota((seqs * hk, 256), 1) // HG_V
    s0bd = jnp.where(r_i == c_i, jnp.tile(s0, (1, 4)), 0.0).astype(BF16)
    own = _iota((rows, seqs * hk), 0) // DEC_SEQ == _iota((rows, seqs * hk), 1) // hk
    o_h = o_h + _mm(jnp.where(own, jnp.tile(q * jnp.exp(b), (1, seqs)), 0.0), s0bd)
    kdt = (k * jnp.exp(tail)).T
    ebt = jnp.exp(b).T
    col_seq = _iota((hk, rows), 1) // DEC_SEQ
    for n in range(seqs):
        upd = _mm(jnp.where(col_seq == n, kdt, 0.0), v)
        last = DEC_SEQ * n + DEC_SEQ - 1
        dcol = ebt[:, last:last + 1]
        for hd in range(HG_HEADS):
            nshg_ref[n, hd] = (dcol[HG_K * hd:HG_K * (hd + 1)] * shg_ref[n, hd]
                               + upd[HG_K * hd:HG_K * (hd + 1), HG_V * hd:HG_V * (hd + 1)])
    hg = _rms_gate(o_h, h[:, C_HG:C_HG + 256], hblk_ref, hgw_ref[...])

    k = _rope(h[:, C_SK:C_SK + 128], cs, ss, ROT_DIM // 2, SWA_HD)
    v = h[:, C_SV:C_SV + 128]
    kb, vb = k.astype(BF16), v.astype(BF16)
    qm = _swa_heads(h[:, C_SQ:C_SQ + 512], lambda t: _rope(t, cs, ss, ROT_DIM // 2, SWA_HD))
    r64 = _iota((64, 1), 0)
    sinkcol = jnp.zeros((64, 1), F32)
    for hd in range(SWA_HEADS):
        sinkcol = jnp.where(r64 // 8 == hd, sinks_ref[layer, hd], sinkcol)
    rc = _iota((64, 128), 0)
    cc = _iota((64, 128), 1)
    first_seq = (rc % 8) < DEC_SEQ
    cmask = cc >= rc % DEC_SEQ
    rn = _iota((64, rows), 0)
    cn = _iota((64, rows), 1)
    sub = _iota((8, 128), 0)
    for pr in range(seqs // 2):
        na, nb = 2 * pr, 2 * pr + 1
        qp = jnp.concatenate([t[8 * pr:8 * pr + 8] for t in qm], axis=0)
        ka, kb_c = ck_ref[na], ck_ref[nb]
        va, vb_c = cv_ref[na], cv_ref[nb]
        sc = jnp.where(first_seq, _mm_nt(qp, ka), _mm_nt(qp, kb_c))
        sc = jnp.where(cmask, sc, NEG)
        nmask = (cn // DEC_SEQ == 2 * pr + (rn % 8) // DEC_SEQ) & (cn % DEC_SEQ <= rn % DEC_SEQ)
        sn = jnp.where(nmask, _mm_nt(qp, kb), NEG)
        m = jnp.maximum(jnp.maximum(jnp.max(sc, axis=1, keepdims=True),
                                    jnp.max(sn, axis=1, keepdims=True)), sinkcol)
        pc = jnp.exp(sc - m)
        pn = jnp.exp(sn - m)
        den = (jnp.sum(pc, axis=1, keepdims=True) + jnp.sum(pn, axis=1, keepdims=True)
               + jnp.exp(sinkcol - m))
        o = (_mm(jnp.where(first_seq, pc, 0.0), va) + _mm(jnp.where(first_seq, 0.0, pc), vb_c)
             + _mm(pn, vb)) / den
        pieces = []
        for hd in range(SWA_HEADS):
            g = hd // SWA_GROUP
            pieces.append(o[8 * hd:8 * hd + 8, SWA_HD * g:SWA_HD * (g + 1)])
        os_scr[8 * pr:8 * pr + 8, :] = jnp.concatenate(pieces, axis=1)
        for new, old_a, old_b, dst in ((k[8 * pr:8 * pr + 8], ka, kb_c, nk_ref),
                                       (v[8 * pr:8 * pr + 8], va, vb_c, nv_ref)):
            ra = pltpu.roll(old_a, WINDOW - DEC_SEQ, 0)
            rb = pltpu.roll(old_b, WINDOW - DEC_SEQ, 0)
            dst[na, 0:120] = ra[0:120]
            dst[na, 120:128] = jnp.where(sub >= DEC_SEQ, pltpu.roll(new, DEC_SEQ, 0), ra[120:128])
            dst[nb, 0:120] = rb[0:120]
            dst[nb, 120:128] = jnp.where(sub >= DEC_SEQ, new, rb[120:128])

    cat = jnp.concatenate([ret, hg, os_scr[...]], axis=1).astype(BF16)
    mix = jnp.dot(cat, wout_ref[...], preferred_element_type=F32)
    x1_ref[...] = _layernorm(ALPHA * x + mix, g1_ref[...], b1_ref[...])


def _sample_mix(layer, x, w_in, w_out, sinks, lbraw, hgw, g1, b1, tabs, state_ret, state_hgrn,
                cache_k, cache_v, consts, gam4):
    r, s = SAMPLE_ROWS, SAMPLE_SEQS
    wspec = lambda a, c: pl.BlockSpec((None, a, c), lambda i: (layer, 0, 0),
                                      pipeline_mode=pl.Buffered(1))
    vspec = lambda c: pl.BlockSpec((None, 1, c), lambda i: (layer, 0, 0))
    in_specs = [pl.BlockSpec(memory_space=pltpu.SMEM),
                pl.BlockSpec((r, D_MODEL), lambda i: (i, 0)),
                wspec(D_MODEL, IN_COLS), wspec(D_MODEL, D_MODEL),
                _const_spec((DEPTH, 256), 1), vspec(256), vspec(D_MODEL), vspec(D_MODEL)]
    in_specs += [_const_spec((r, LANES), 1)] * 4
    in_specs += [pl.BlockSpec((None, s, RET_HEADS, RET_QK, RET_V), lambda i: (layer, i, 0, 0, 0)),
                 pl.BlockSpec((None, s, HG_HEADS, HG_K, HG_V), lambda i: (layer, i, 0, 0, 0)),
                 pl.BlockSpec((None, s, WINDOW, 128), lambda i: (layer, i, 0, 0)),
                 pl.BlockSpec((None, s, WINDOW, 128), lambda i: (layer, i, 0, 0))]
    in_specs += [_const_spec(c.shape, 1) for c in consts]
    out_shape = [jax.ShapeDtypeStruct((DEC_BATCH * DEC_SEQ, D_MODEL), F32),
                 jax.ShapeDtypeStruct((DEC_BATCH, RET_HEADS, RET_QK, RET_V), F32),
                 jax.ShapeDtypeStruct((DEC_BATCH, HG_HEADS, HG_K, HG_V), F32),
                 jax.ShapeDtypeStruct((DEC_BATCH, WINDOW, 128), F32),
                 jax.ShapeDtypeStruct((DEC_BATCH, WINDOW, 128), F32)]
    out_specs = [pl.BlockSpec((r, D_MODEL), lambda i: (i, 0)),
                 pl.BlockSpec((s, RET_HEADS, RET_QK, RET_V), lambda i: (i, 0, 0, 0)),
                 pl.BlockSpec((s, HG_HEADS, HG_K, HG_V), lambda i: (i, 0, 0, 0)),
                 pl.BlockSpec((s, WINDOW, 128), lambda i: (i, 0, 0)),
                 pl.BlockSpec((s, WINDOW, 128), lambda i: (i, 0, 0))]
    return pl.pallas_call(
        functools.partial(_sample_mix_kernel, layer=layer, gam4=gam4),
        grid=(DEC_BATCH // s,), in_specs=in_specs, out_specs=out_specs, out_shape=out_shape,
        scratch_shapes=[pltpu.VMEM((r, 512), F32)], name=f"sample_mix_{layer}",
        compiler_params=pltpu.CompilerParams(dimension_semantics=("arbitrary",),
                                             vmem_limit_bytes=VMEM_LIMIT),
    )(sinks, x, w_in, w_out, lbraw, hgw, g1, b1, *tabs, state_ret, state_hgrn, cache_k, cache_v,
      *consts)


def _decay_levels(rows, seqlen, levels):
    t = np.arange(rows)[:, None]
    u = np.arange(rows)[None, :]
    same = (t // seqlen) == (u // seqlen)
    blocks = [same & (u <= t)]
    for l in range(levels):
        m = 1 << l
        r = (t // (2 * m)) * (2 * m) + m
        lower = t >= r
        blocks.append(np.where(lower, (u >= r) & (u <= t), (u > t) & (u <= r - 1)))
    blocks.append(same & (u > t))
    return jnp.asarray(np.concatenate(blocks, axis=0).astype(np.float32), dtype=BF16)


def _rope_tables(pos):
    pos = pos.astype(F32)[:, None]
    inv = RET_THETA ** (-jnp.linspace(0.0, 1.0, RET_QK // 2, dtype=F32))
    ang = pos * inv[None, :]
    cr = jnp.tile(jnp.concatenate([jnp.cos(ang), jnp.cos(ang)], -1), (1, RET_HEADS))
    sr = jnp.tile(jnp.concatenate([-jnp.sin(ang), jnp.sin(ang)], -1), (1, RET_HEADS))
    inv = ROPE_THETA ** (-jnp.arange(0, ROT_DIM, 2, dtype=F32) / ROT_DIM)
    ang = pos * inv[None, :]
    rest = SWA_HD - ROT_DIM
    ones = jnp.ones((ang.shape[0], rest), F32)
    zeros = jnp.zeros((ang.shape[0], rest), F32)
    cs = jnp.tile(jnp.concatenate([jnp.cos(ang), jnp.cos(ang), ones], -1), (1, 2))
    ss = jnp.tile(jnp.concatenate([-jnp.sin(ang), jnp.sin(ang), zeros], -1), (1, 2))
    return cr, sr, cs, ss


def _ret_tables(tok, same):
    lg = jnp.log(1.0 - 2.0 ** (-5.0 - jnp.arange(RET_HEADS, dtype=F32)))
    n_tok = int(tok.max()) + 1
    tokf = jnp.asarray(tok, F32)
    diff = tokf[:, None] - tokf[None, :]
    dec = jnp.where(jnp.asarray(same) & (diff >= 0),
                    jnp.exp(jnp.maximum(diff, 0.0)[None] * lg[:, None, None]), 0.0)
    inter = jnp.repeat(jnp.exp((tokf + 1.0)[:, None] * lg[None, :]), RET_V, axis=1)
    kdec = jnp.repeat(jnp.exp((n_tok - 1.0 - tokf)[:, None] * lg[None, :]), RET_QK, axis=1)
    total = jnp.exp(n_tok * lg)
    return dec, inter, kdec, total


def kernel(x_prompt, x_sample, state_ret, state_hgrn, cache_swa_k, cache_swa_v, w_in, w_out,
           swa_sinks, hg_lb_raw, hg_norm_w, ln_mix_g, ln_mix_b, w_up, w_down, ln_ffn_g, ln_ffn_b):
    w_in_b, w_out_b = w_in.astype(BF16), w_out.astype(BF16)
    w_up_b, w_down_b = w_up.astype(BF16), w_down.astype(BF16)
    hgw = jnp.tile(hg_norm_w, (1, HG_HEADS)).reshape(DEPTH, 1, HG_HEADS * HG_V)
    g1, b1 = ln_mix_g.reshape(DEPTH, 1, D_MODEL), ln_mix_b.reshape(DEPTH, 1, D_MODEL)
    g2, b2 = ln_ffn_g.reshape(DEPTH, 1, D_MODEL), ln_ffn_b.reshape(DEPTH, 1, D_MODEL)
    cache_k = cache_swa_k.reshape(DEPTH, DEC_BATCH, WINDOW, 128)
    cache_v = cache_swa_v.reshape(DEPTH, DEC_BATCH, WINDOW, 128)

    hblk = jnp.asarray((np.arange(256)[:, None] // 64 == np.arange(256)[None, :] // 64)
                       .astype(np.float32), dtype=BF16)
    p_tabs = _rope_tables(jnp.arange(SEQ))
    tok = np.arange(CHUNK)
    dec, inter, kdec, total = _ret_tables(tok, np.ones((CHUNK, CHUNK), bool))
    rowdec = jnp.broadcast_to(jnp.repeat(total, RET_QK)[:, None], (128, 256))
    p_consts = (dec, inter, kdec, rowdec, _decay_levels(CHUNK, CHUNK, PROMPT_LEVELS), hblk)
    row = np.arange(SAMPLE_ROWS)
    s_tabs = _rope_tables(PAST_LEN + jnp.asarray(row % DEC_SEQ))
    same = (row[:, None] // DEC_SEQ) == (row[None, :] // DEC_SEQ)
    dec_s, inter_s, kdec_s, total_s = _ret_tables(row % DEC_SEQ, same)
    s_consts = (dec_s, inter_s, kdec_s, _decay_levels(SAMPLE_ROWS, DEC_SEQ, SAMPLE_LEVELS), hblk)
    gam4 = tuple(float((1.0 - 2.0 ** (-5.0 - h)) ** DEC_SEQ) for h in range(RET_HEADS))
    del total_s

    xp = x_prompt
    xs = x_sample.reshape(DEC_BATCH * DEC_SEQ, D_MODEL)
    outs = [[] for _ in range(8)]
    for l in range(DEPTH):
        x1, st_r, st_h, pk, pv = _prompt_mix(l, xp, w_in_b, w_out_b, swa_sinks, hg_lb_raw, hgw,
                                             g1, b1, p_tabs, p_consts)
        xp = _ffn(l, x1.reshape(BATCH * SEQ, D_MODEL), w_up_b, w_down_b, g2, b2)
        xp = xp.reshape(BATCH, SEQ, D_MODEL)
        outs[0].append(jnp.stack([st_r[:, RET_QK * h:RET_QK * (h + 1), RET_V * h:RET_V * (h + 1)]
                                  for h in range(RET_HEADS)], axis=1))
        outs[1].append(jnp.stack([st_h[:, HG_K * h:HG_K * (h + 1), HG_V * h:HG_V * (h + 1)]
                                  for h in range(HG_HEADS)], axis=1))
        outs[2].append(pk.reshape(BATCH, WINDOW, SWA_KV, SWA_HD))
        outs[3].append(pv.reshape(BATCH, WINDOW, SWA_KV, SWA_HD))

        x1, ns_r, ns_h, nk, nv = _sample_mix(l, xs, w_in_b, w_out_b, swa_sinks, hg_lb_raw, hgw,
                                             g1, b1, s_tabs, state_ret, state_hgrn, cache_k,
                                             cache_v, s_consts, gam4)
        xs = _ffn(l, x1, w_up_b, w_down_b, g2, b2)
        outs[4].append(ns_r)
        outs[5].append(ns_h)
        outs[6].append(nk.reshape(DEC_BATCH, WINDOW, SWA_KV, SWA_HD))
        outs[7].append(nv.reshape(DEC_BATCH, WINDOW, SWA_KV, SWA_HD))

    return (xp, xs.reshape(DEC_BATCH, DEC_SEQ, D_MODEL)) + tuple(jnp.stack(o) for o in outs)
```

```python
import functools

import numpy as np
import jax
import jax.numpy as jnp
from jax import lax
from jax.experimental import pallas as pl
from jax.experimental.pallas import tpu as pltpu

F32 = jnp.float32
BF16 = jnp.bfloat16

D_MODEL = 1024
BATCH = 2
SEQ = 8192
DEPTH = 4
DEC_BATCH = 128
DEC_SEQ = 4
PAST_LEN = 8192
RET_HEADS = 4
RET_QK = 32
RET_V = 64
RET_THETA = 10000.0
HG_HEADS = 4
HG_K = 64
HG_V = 64
SWA_HEADS = 8
SWA_KV = 2
SWA_GROUP = SWA_HEADS // SWA_KV
SWA_HD = 64
WINDOW = 128
ROT_DIM = 16
ROPE_THETA = 500000.0
CHUNK = 128
D_FF = 4 * D_MODEL
IN_COLS = 2560
ALPHA = (2.0 * DEPTH) ** 0.25
LN_EPS = 1e-5
RMS_EPS = 1e-6
NEG = -1e30

C_RQ, C_RK, C_RV, C_RG = 0, 128, 256, 512
C_HQ, C_HF, C_HI, C_HG = 768, 1024, 1280, 1536
C_SQ, C_SK, C_SV = 1792, 2304, 2432

LANES = 128
PROMPT_TILE = 512
FFN_TILE = 512
SAMPLE_SEQS = 16
SAMPLE_ROWS = SAMPLE_SEQS * DEC_SEQ
PROMPT_LEVELS = 7
SAMPLE_LEVELS = 2
VMEM_LIMIT = 56 * 1024 * 1024


def _mm(a, b):
    return jnp.dot(a.astype(BF16), b.astype(BF16), preferred_element_type=F32)


def _mm_nt(a, b):
    return lax.dot_general(a.astype(BF16), b.astype(BF16), (((1,), (1,)), ((), ())),
                           preferred_element_type=F32)


def _iota(shape, dim):
    return lax.broadcasted_iota(jnp.int32, shape, dim)


def _layernorm(x, g, b):
    mu = jnp.mean(x, axis=-1, keepdims=True)
    xc = x - mu
    var = jnp.mean(xc * xc, axis=-1, keepdims=True)
    return xc * lax.rsqrt(var + LN_EPS) * g + b


def _rope(x, cos_t, sin_t, half, period):
    first_half = (_iota((1, LANES), 1) % period) < half
    fwd = pltpu.roll(x, LANES - half, 1)
    bwd = pltpu.roll(x, half, 1)
    return x * cos_t + jnp.where(first_half, fwd, bwd) * sin_t


def _silu(x):
    return x * (1.0 / (1.0 + jnp.exp(-x)))


def _split3(x):
    hi = x.astype(BF16)
    r = x - hi.astype(F32)
    mid = r.astype(BF16)
    lo = (r - mid.astype(F32)).astype(BF16)
    return hi, mid, lo


def _hg_lower_bound(lbraw_ref, layer):
    raw = lbraw_ref[...]
    e = jnp.exp(raw - jnp.max(raw, axis=0, keepdims=True))
    sm = e / jnp.sum(e, axis=0, keepdims=True)
    cum = sm[0:1]
    for i in range(1, layer + 1):
        cum = cum + sm[i:i + 1]
    return cum - sm[0:1]


def _hg_log_forget(hf, lb):
    logsig = jnp.minimum(hf, 0.0) - jnp.log(1.0 + jnp.exp(-jnp.abs(hf)))
    a = jnp.log(lb)
    b = jnp.log(1.0 - lb) + logsig
    return jnp.maximum(a, b) + jnp.log(1.0 + jnp.exp(-jnp.abs(a - b)))


def _level_exponents(b, rows, levels):
    g = rows // 8
    b3 = b.reshape(g, 8, 256)
    sub = _iota((g, 8, 256), 1)
    dn1 = pltpu.roll(b3, 1, 1)
    dn2 = pltpu.roll(b3, 2, 1)
    up1 = pltpu.roll(b3, 7, 1)
    out = [jnp.where(sub % 2 == 1, b3 - dn1, 0.0)]
    m4 = sub % 4
    out.append(jnp.where(m4 == 0, up1 - b3,
                         jnp.where(m4 == 1, 0.0, jnp.where(m4 == 2, b3 - dn1, b3 - dn2))))
    if levels > 2:
        r3 = jnp.broadcast_to(b3[:, 3:4, :], b3.shape)
        out.append(jnp.where(sub < 4, r3 - b3, b3 - r3))
        r7 = jnp.broadcast_to(b3[:, 7:8, :], b3.shape)
        grp = _iota((g, 8, 256), 0)
        for l in range(3, levels):
            span = (2 << l) // 8
            half = span // 2
            ref = r7.reshape(g // span, span, 8, 256)[:, half - 1:half]
            ref = jnp.broadcast_to(ref, (g // span, span, 8, 256)).reshape(g, 8, 256)
            out.append(jnp.where(grp % span >= half, b3 - ref, ref - b3))
    return [e.reshape(rows, 256) for e in out]


def _head_blockdiag(v, rows):
    r = _iota((4 * rows, 256), 0) // rows
    c = _iota((4 * rows, 256), 1) // 64
    return jnp.where(r == c, jnp.tile(v, (4, 1)), 0.0).astype(BF16)


def _rms_gate(o, gate, hblk_ref, w=None):
    ms = _mm(o * o, hblk_ref[...]) * (1.0 / 64.0)
    y = o * lax.rsqrt(ms + RMS_EPS)
    if w is not None:
        y = y * w
    return y * _silu(gate)


def _ret_intra(q, k, v, dec_ref, rows):
    lane_head = _iota((rows, 128), 1) // RET_QK
    a = [_mm_nt(jnp.where(lane_head == h, q, 0.0), k) * dec_ref[h] for h in range(RET_HEADS)]
    return _mm(jnp.concatenate(a, axis=1), _head_blockdiag(v, rows))


def _hgrn_intra(q, k, v, lf, mall_ref, hblk_ref, rows, levels):
    l3 = jnp.concatenate(_split3(lf), axis=1)
    eb = jnp.dot(mall_ref[...], l3, preferred_element_type=F32)
    eb = eb[:, 0:256] + eb[:, 256:512] + eb[:, 512:768]
    b = eb[0:rows]
    tail = eb[(levels + 1) * rows:(levels + 2) * rows]
    row = _iota((rows, 256), 0)
    lane_head = _iota((rows, 256), 1) // HG_K
    ra = _iota((rows, rows), 0)
    ca = _iota((rows, rows), 1)
    acc = [jnp.zeros((rows, rows), F32) for _ in range(HG_HEADS)]
    for l in range(levels):
        w = jnp.exp(eb[(l + 1) * rows:(l + 2) * rows])
        lower = ((row >> l) & 1) == 1
        qs = jnp.where(lower, q * w, 0.0).astype(BF16)
        ks = jnp.where(lower, 0.0, k * w).astype(BF16)
        same_block = (ra >> (l + 1)) == (ca >> (l + 1))
        for h in range(HG_HEADS):
            p = _mm_nt(jnp.where(lane_head == h, qs, jnp.zeros_like(qs)), ks)
            acc[h] = acc[h] + jnp.where(same_block, p, 0.0)
    o = _mm(jnp.concatenate(acc, axis=1), _head_blockdiag(v, rows))
    o = o + _mm(q * k, hblk_ref[...]) * v
    return o, b, tail


def _swa_heads(qcols, rope_q):
    out = []
    for c in range(4):
        qc = rope_q(qcols[:, LANES * c:LANES * (c + 1)]) * (SWA_HD ** -0.5)
        half = _iota(qc.shape, 1) // SWA_HD
        for p in range(2):
            g = (2 * c + p) // SWA_GROUP
            qm = jnp.where(half == p, qc, 0.0)
            if p != g:
                qm = pltpu.roll(qm, SWA_HD, 1)
            out.append(qm)
    return out


def _prompt_mix_kernel(sinks_ref, x_ref, win_ref, wout_ref, lbraw_ref, hgw_ref, g1_ref, b1_ref,
                       cr_ref, sr_ref, cs_ref, ss_ref, dec_ref, inter_ref, kdec_ref, rowdec_ref,
                       tri_ref, hblk_ref, lvl_ref, band_ref,
                       x1_ref, str_ref, sth_ref, pk_ref, pv_ref,
                       h_scr, cat_scr, sret, shg, kprev, vprev, qs_scr, ks_scr, a_scr, *, layer):
    j = pl.program_id(1)

    @pl.when(j == 0)
    def _():
        sret[...] = jnp.zeros_like(sret)
        shg[...] = jnp.zeros_like(shg)
        kprev[...] = jnp.zeros_like(kprev)
        vprev[...] = jnp.zeros_like(vprev)

    h_scr[...] = jnp.dot(x_ref[...].astype(BF16), win_ref[...], preferred_element_type=F32)
    lb = _hg_lower_bound(lbraw_ref, layer)
    n_chunks = PROMPT_TILE // CHUNK
    levels = PROMPT_LEVELS

    def chunk(c, carry):
        r0 = pl.multiple_of(c * CHUNK, CHUNK)
        rows = pl.ds(r0, CHUNK)
        cr, sr = cr_ref[rows, :], sr_ref[rows, :]
        cs, ss = cs_ref[rows, :], ss_ref[rows, :]

        q = _rope(h_scr[rows, C_RQ:C_RQ + 128], cr, sr, RET_QK // 2, RET_QK)
        k = _rope(h_scr[rows, C_RK:C_RK + 128], cr, sr, RET_QK // 2, RET_QK) * (RET_QK ** -0.5)
        v = h_scr[rows, C_RV:C_RV + 256]
        lane_head = _iota((CHUNK, 128), 1) // RET_QK
        for hd in range(RET_HEADS):
            a = _mm_nt(jnp.where(lane_head == hd, q, 0.0), k) * dec_ref[hd]
            a_scr[:, CHUNK * hd:CHUNK * (hd + 1)] = a.astype(BF16)
        o_r = jnp.dot(a_scr[...], _head_blockdiag(v, CHUNK), preferred_element_type=F32)
        s0 = sret[...]
        o_r = o_r + _mm(q, s0) * inter_ref[...]
        r_i = _iota((128, 256), 0) // RET_QK
        c_i = _iota((128, 256), 1) // RET_V
        upd = _mm((k * kdec_ref[...]).T, v)
        sret[...] = rowdec_ref[...] * s0 + jnp.where(r_i == c_i, upd, 0.0)
        ret = _rms_gate(o_r, h_scr[rows, C_RG:C_RG + 256], hblk_ref)

        q = h_scr[rows, C_HQ:C_HQ + 256]
        lf = _hg_log_forget(h_scr[rows, C_HF:C_HF + 256], lb)
        k = 1.0 - jnp.exp(lf)
        v = h_scr[rows, C_HI:C_HI + 256]
        l3 = jnp.concatenate(_split3(lf), axis=1)
        b = jnp.dot(tri_ref[...], l3, preferred_element_type=F32)
        b = b[:, 0:256] + b[:, 256:512] + b[:, 512:768]
        for l, e in enumerate(_level_exponents(b, CHUNK, levels)):
            w = jnp.exp(e)
            qs_scr[l] = (q * w).astype(BF16)
            ks_scr[l] = (k * w).astype(BF16)
        qs_scr[levels] = q.astype(BF16)
        ks_scr[levels] = k.astype(BF16)
        lvl = lvl_ref[...]
        for hd in range(HG_HEADS):
            lanes = slice(HG_K * hd, HG_K * (hd + 1))
            a = jnp.zeros((CHUNK, CHUNK), F32)
            for l in range(levels + 1):
                p = lax.dot_general(qs_scr[l, :, lanes], ks_scr[l, :, lanes],
                                    (((1,), (1,)), ((), ())), preferred_element_type=F32)
                a = jnp.where(lvl == l, p, a)
            a_scr[:, CHUNK * hd:CHUNK * (hd + 1)] = a.astype(BF16)
        o_h = jnp.dot(a_scr[...], _head_blockdiag(v, CHUNK), preferred_element_type=F32)
        s0 = shg[...]
        o_h = o_h + _mm(q * jnp.exp(b), s0)
        bl = b[CHUNK - 1:CHUNK, :]
        decay_t = jnp.broadcast_to(jnp.exp(bl), (128, 256)).T
        r_i = _iota((256, 256), 0) // HG_K
        c_i = _iota((256, 256), 1) // HG_V
        upd = _mm((k * jnp.exp(bl - b)).T, v)
        shg[...] = jnp.tile(decay_t, (1, 2)) * s0 + jnp.where(r_i == c_i, upd, 0.0)
        hg = _rms_gate(o_h, h_scr[rows, C_HG:C_HG + 256], hblk_ref, hgw_ref[...])

        k = _rope(h_scr[rows, C_SK:C_SK + 128], cs, ss, ROT_DIM // 2, SWA_HD)
        v = h_scr[rows, C_SV:C_SV + 128]
        kk = jnp.concatenate([kprev[...], k.astype(BF16)], axis=0)
        vv = jnp.concatenate([vprev[...], v.astype(BF16)], axis=0)
        mask = band_ref[...] > jnp.where(j * n_chunks + c > 0, 0, 1)
        qm = _swa_heads(h_scr[rows, C_SQ:C_SQ + 512],
                        lambda t: _rope(t, cs, ss, ROT_DIM // 2, SWA_HD))
        outs = []
        for hd in range(SWA_HEADS):
            g = hd // SWA_GROUP
            sink = sinks_ref[layer, hd]
            s = jnp.where(mask, _mm_nt(qm[hd], kk), NEG)
            m = jnp.maximum(jnp.max(s, axis=1, keepdims=True), sink)
            p = jnp.exp(s - m)
            den = jnp.sum(p, axis=1, keepdims=True) + jnp.exp(sink - m)
            outs.append(_mm(p, vv)[:, SWA_HD * g:SWA_HD * (g + 1)] / den)
        kprev[...] = k.astype(BF16)
        vprev[...] = v.astype(BF16)
        pk_ref[...] = k.T
        pv_ref[...] = v.T

        cat_scr[rows, :] = jnp.concatenate([ret, hg] + outs, axis=1).astype(BF16)
        return carry

    lax.fori_loop(0, n_chunks, chunk, 0)
    mix = jnp.dot(cat_scr[...], wout_ref[...], preferred_element_type=F32)
    x1_ref[...] = _layernorm(ALPHA * x_ref[...] + mix, g1_ref[...], b1_ref[...])
    str_ref[...] = sret[...]
    sth_ref[...] = shg[...]


def _const_spec(shape, ngrid):
    zeros = (0,) * len(shape)
    if ngrid == 1:
        return pl.BlockSpec(shape, lambda i: zeros)
    return pl.BlockSpec(shape, lambda b, j: zeros)


def _prompt_mix(layer, x, w_in, w_out, sinks, lbraw, hgw, g1, b1, tabs, consts):
    t = PROMPT_TILE
    wspec = lambda r, c: pl.BlockSpec((None, r, c), lambda b, j: (layer, 0, 0),
                                      pipeline_mode=pl.Buffered(1))
    vspec = lambda c: pl.BlockSpec((None, 1, c), lambda b, j: (layer, 0, 0))
    tspec = pl.BlockSpec((t, LANES), lambda b, j: (j, 0))
    in_specs = [pl.BlockSpec(memory_space=pltpu.SMEM),
                pl.BlockSpec((None, t, D_MODEL), lambda b, j: (b, j, 0)),
                wspec(D_MODEL, IN_COLS), wspec(D_MODEL, D_MODEL),
                _const_spec((DEPTH, 256), 2), vspec(256), vspec(D_MODEL), vspec(D_MODEL),
                tspec, tspec, tspec, tspec] + [_const_spec(c.shape, 2) for c in consts]
    out_shape = [jax.ShapeDtypeStruct((BATCH, SEQ, D_MODEL), F32),
                 jax.ShapeDtypeStruct((BATCH, 128, 256), F32),
                 jax.ShapeDtypeStruct((BATCH, 256, 256), F32),
                 jax.ShapeDtypeStruct((BATCH, WINDOW, 128), F32),
                 jax.ShapeDtypeStruct((BATCH, WINDOW, 128), F32)]
    out_specs = [pl.BlockSpec((None, t, D_MODEL), lambda b, j: (b, j, 0)),
                 pl.BlockSpec((None, 128, 256), lambda b, j: (b, 0, 0)),
                 pl.BlockSpec((None, 256, 256), lambda b, j: (b, 0, 0)),
                 pl.BlockSpec((None, WINDOW, 128), lambda b, j: (b, 0, 0)),
                 pl.BlockSpec((None, WINDOW, 128), lambda b, j: (b, 0, 0))]
    scratch = [pltpu.VMEM((t, IN_COLS), F32), pltpu.VMEM((t, D_MODEL), BF16),
               pltpu.VMEM((128, 256), F32), pltpu.VMEM((256, 256), F32),
               pltpu.VMEM((WINDOW, 128), BF16), pltpu.VMEM((WINDOW, 128), BF16),
               pltpu.VMEM((PROMPT_LEVELS + 1, CHUNK, 256), BF16),
               pltpu.VMEM((PROMPT_LEVELS + 1, CHUNK, 256), BF16),
               pltpu.VMEM((CHUNK, 4 * CHUNK), BF16)]
    return pl.pallas_call(
        functools.partial(_prompt_mix_kernel, layer=layer),
        grid=(BATCH, SEQ // t), in_specs=in_specs, out_specs=out_specs, out_shape=out_shape,
        scratch_shapes=scratch, name=f"prompt_mix_{layer}",
        compiler_params=pltpu.CompilerParams(dimension_semantics=("arbitrary", "arbitrary"),
                                             vmem_limit_bytes=VMEM_LIMIT),
    )(sinks, x, w_in, w_out, lbraw, hgw, g1, b1, *tabs, *consts)


def _ffn_kernel(x_ref, wup_ref, wdn_ref, g_ref, b_ref, o_ref):
    x = x_ref[...]
    xb = x.astype(BF16)
    acc = jnp.zeros(x.shape, F32)
    step = 1024
    for i in range(D_FF // step):
        hid = jnp.maximum(jnp.dot(xb, wup_ref[:, i * step:(i + 1) * step],
                                  preferred_element_type=F32), 0.0)
        acc = acc + jnp.dot((hid * hid).astype(BF16), wdn_ref[i * step:(i + 1) * step, :],
                            preferred_element_type=F32)
    o_ref[...] = _layernorm(ALPHA * x + acc, g_ref[...], b_ref[...])


def _ffn(layer, x, w_up, w_down, g2, b2):
    rows = x.shape[0]
    t = FFN_TILE
    return pl.pallas_call(
        _ffn_kernel,
        grid=(rows // t,),
        in_specs=[pl.BlockSpec((t, D_MODEL), lambda i: (i, 0)),
                  pl.BlockSpec((None, D_MODEL, D_FF), lambda i: (layer, 0, 0),
                               pipeline_mode=pl.Buffered(1)),
                  pl.BlockSpec((None, D_FF, D_MODEL), lambda i: (layer, 0, 0),
                               pipeline_mode=pl.Buffered(1)),
                  pl.BlockSpec((None, 1, D_MODEL), lambda i: (layer, 0, 0)),
                  pl.BlockSpec((None, 1, D_MODEL), lambda i: (layer, 0, 0))],
        out_specs=pl.BlockSpec((t, D_MODEL), lambda i: (i, 0)),
        out_shape=jax.ShapeDtypeStruct((rows, D_MODEL), F32),
        name=f"ffn_{layer}_{rows}",
        compiler_params=pltpu.CompilerParams(dimension_semantics=("arbitrary",),
                                             vmem_limit_bytes=VMEM_LIMIT),
    )(x, w_up, w_down, g2, b2)


def _sample_mix_kernel(sinks_ref, x_ref, win_ref, wout_ref, lbraw_ref, hgw_ref, g1_ref, b1_ref,
                       cr_ref, sr_ref, cs_ref, ss_ref, sret_ref, shg_ref, ck_ref, cv_ref,
                       dec_ref, inter_ref, kdec_ref, mall_ref, hblk_ref,
                       x1_ref, nsret_ref, nshg_ref, nk_ref, nv_ref, os_scr, *, layer, gam4):
    rows, seqs = SAMPLE_ROWS, SAMPLE_SEQS
    x = x_ref[...]
    h = jnp.dot(x.astype(BF16), win_ref[...], preferred_element_type=F32)
    cr, sr, cs, ss = cr_ref[...], sr_ref[...], cs_ref[...], ss_ref[...]
    lb = _hg_lower_bound(lbraw_ref, layer)

    q = _rope(h[:, C_RQ:C_RQ + 128], cr, sr, RET_QK // 2, RET_QK)
    k = _rope(h[:, C_RK:C_RK + 128], cr, sr, RET_QK // 2, RET_QK) * (RET_QK ** -0.5)
    v = h[:, C_RV:C_RV + 256]
    o_r = _ret_intra(q, k, v, dec_ref, rows)
    hk = RET_HEADS * RET_QK
    s0 = sret_ref[...].reshape(seqs * hk, RET_V)
    r_i = (_iota((seqs * hk, 256), 0) % hk) // RET_QK
    c_i = _iota((seqs * hk, 256), 1) // RET_V
    s0bd = jnp.where(r_i == c_i, jnp.tile(s0, (1, 4)), 0.0).astype(BF16)
    own = _iota((rows, seqs * hk), 0) // DEC_SEQ == _iota((rows, seqs * hk), 1) // hk
    o_r = o_r + _mm(jnp.where(own, jnp.tile(q, (1, seqs)), 0.0), s0bd) * inter_ref[...]
    kdt = (k * kdec_ref[...]).T
    col_seq = _iota((hk, rows), 1) // DEC_SEQ
    for n in range(seqs):
        upd = _mm(jnp.where(col_seq == n, kdt, 0.0), v)
        for hd in range(RET_HEADS):
            nsret_ref[n, hd] = (gam4[hd] * sret_ref[n, hd]
                                + upd[RET_QK * hd:RET_QK * (hd + 1), RET_V * hd:RET_V * (hd + 1)])
    ret = _rms_gate(o_r, h[:, C_RG:C_RG + 256], hblk_ref)

    q = h[:, C_HQ:C_HQ + 256]
    lf = _hg_log_forget(h[:, C_HF:C_HF + 256], lb)
    k = 1.0 - jnp.exp(lf)
    v = h[:, C_HI:C_HI + 256]
    o_h, b, tail = _hgrn_intra(q, k, v, lf, mall_ref, hblk_ref, rows, SAMPLE_LEVELS)
    hk = HG_HEADS * HG_K
    s0 = shg_ref[...].reshape(seqs * hk, HG_V)
    r_i = (_iota((seqs * hk, 256), 0) % hk) // HG_K
    c_i = _iota((seqs * hk, 256), 1) // HG_V
    s0bd = jnp.where(r_i == c_i, jnp.tile(s0, (1, 4)), 0.0).astype(BF16)
    own = _iota((rows, seqs * hk), 0) // DEC_SEQ == _iota((rows, seqs * hk), 1) // hk
    o_h = o_h + _mm(jnp.where(own, jnp.tile(q * jnp.exp(b), (1, seqs)), 0.0), s0bd)
    kdt = (k * jnp.exp(tail)).T
    ebt = jnp.exp(b).T
    col_seq = _iota((hk, rows), 1) // DEC_SEQ
    for n in range(seqs):
        upd = _mm(jnp.where(col_seq == n, kdt, 0.0), v)
        last = DEC_SEQ * n + DEC_SEQ - 1
        dcol = ebt[:, last:last + 1]
        for hd in range(HG_HEADS):
            nshg_ref[n, hd] = (dcol[HG_K * hd:HG_K * (hd + 1)] * shg_ref[n, hd]
                               + upd[HG_K * hd:HG_K * (hd + 1), HG_V * hd:HG_V * (hd + 1)])
    hg = _rms_gate(o_h, h[:, C_HG:C_HG + 256], hblk_ref, hgw_ref[...])

    k = _rope(h[:, C_SK:C_SK + 128], cs, ss, ROT_DIM // 2, SWA_HD)
    v = h[:, C_SV:C_SV + 128]
    kb, vb = k.astype(BF16), v.astype(BF16)
    qm = _swa_heads(h[:, C_SQ:C_SQ + 512], lambda t: _rope(t, cs, ss, ROT_DIM // 2, SWA_HD))
    r64 = _iota((64, 1), 0)
    sinkcol = jnp.zeros((64, 1), F32)
    for hd in range(SWA_HEADS):
        sinkcol = jnp.where(r64 // 8 == hd, sinks_ref[layer, hd], sinkcol)
    rc = _iota((64, 128), 0)
    cc = _iota((64, 128), 1)
    first_seq = (rc % 8) < DEC_SEQ
    cmask = cc >= rc % DEC_SEQ
    rn = _iota((64, rows), 0)
    cn = _iota((64, rows), 1)
    sub = _iota((8, 128), 0)
    for pr in range(seqs // 2):
        na, nb = 2 * pr, 2 * pr + 1
        qp = jnp.concatenate([t[8 * pr:8 * pr + 8] for t in qm], axis=0)
        ka, kb_c = ck_ref[na], ck_ref[nb]
        va, vb_c = cv_ref[na], cv_ref[nb]
        sc = jnp.where(first_seq, _mm_nt(qp, ka), _mm_nt(qp, kb_c))
        sc = jnp.where(cmask, sc, NEG)
        nmask = (cn // DEC_SEQ == 2 * pr + (rn % 8) // DEC_SEQ) & (cn % DEC_SEQ <= rn % DEC_SEQ)
        sn = jnp.where(nmask, _mm_nt(qp, kb), NEG)
        m = jnp.maximum(jnp.maximum(jnp.max(sc, axis=1, keepdims=True),
                                    jnp.max(sn, axis=1, keepdims=True)), sinkcol)
        pc = jnp.exp(sc - m)
        pn = jnp.exp(sn - m)
        den = (jnp.sum(pc, axis=1, keepdims=True) + jnp.sum(pn, axis=1, keepdims=True)
               + jnp.exp(sinkcol - m))
        o = (_mm(jnp.where(first_seq, pc, 0.0), va) + _mm(jnp.where(first_seq, 0.0, pc), vb_c)
             + _mm(pn, vb)) / den
        pieces = []
        for hd in range(SWA_HEADS):
            g = hd // SWA_GROUP
            pieces.append(o[8 * hd:8 * hd + 8, SWA_HD * g:SWA_HD * (g + 1)])
        os_scr[8 * pr:8 * pr + 8, :] = jnp.concatenate(pieces, axis=1)
        for new, old_a, old_b, dst in ((k[8 * pr:8 * pr + 8], ka, kb_c, nk_ref),
                                       (v[8 * pr:8 * pr + 8], va, vb_c, nv_ref)):
            ra = pltpu.roll(old_a, WINDOW - DEC_SEQ, 0)
            rb = pltpu.roll(old_b, WINDOW - DEC_SEQ, 0)
            dst[na, 0:120] = ra[0:120]
            dst[na, 120:128] = jnp.where(sub >= DEC_SEQ, pltpu.roll(new, DEC_SEQ, 0), ra[120:128])
            dst[nb, 0:120] = rb[0:120]
            dst[nb, 120:128] = jnp.where(sub >= DEC_SEQ, new, rb[120:128])

    cat = jnp.concatenate([ret, hg, os_scr[...]], axis=1).astype(BF16)
    mix = jnp.dot(cat, wout_ref[...], preferred_element_type=F32)
    x1_ref[...] = _layernorm(ALPHA * x + mix, g1_ref[...], b1_ref[...])


def _sample_mix(layer, x, w_in, w_out, sinks, lbraw, hgw, g1, b1, tabs, state_ret, state_hgrn,
                cache_k, cache_v, consts, gam4):
    r, s = SAMPLE_ROWS, SAMPLE_SEQS
    wspec = lambda a, c: pl.BlockSpec((None, a, c), lambda i: (layer, 0, 0),
                                      pipeline_mode=pl.Buffered(1))
    vspec = lambda c: pl.BlockSpec((None, 1, c), lambda i: (layer, 0, 0))
    in_specs = [pl.BlockSpec(memory_space=pltpu.SMEM),
                pl.BlockSpec((r, D_MODEL), lambda i: (i, 0)),
                wspec(D_MODEL, IN_COLS), wspec(D_MODEL, D_MODEL),
                _const_spec((DEPTH, 256), 1), vspec(256), vspec(D_MODEL), vspec(D_MODEL)]
    in_specs += [_const_spec((r, LANES), 1)] * 4
    in_specs += [pl.BlockSpec((None, s, RET_HEADS, RET_QK, RET_V), lambda i: (layer, i, 0, 0, 0)),
                 pl.BlockSpec((None, s, HG_HEADS, HG_K, HG_V), lambda i: (layer, i, 0, 0, 0)),
                 pl.BlockSpec((None, s, WINDOW, 128), lambda i: (layer, i, 0, 0)),
                 pl.BlockSpec((None, s, WINDOW, 128), lambda i: (layer, i, 0, 0))]
    in_specs += [_const_spec(c.shape, 1) for c in consts]
    out_shape = [jax.ShapeDtypeStruct((DEC_BATCH * DEC_SEQ, D_MODEL), F32),
                 jax.ShapeDtypeStruct((DEC_BATCH, RET_HEADS, RET_QK, RET_V), F32),
                 jax.ShapeDtypeStruct((DEC_BATCH, HG_HEADS, HG_K, HG_V), F32),
                 jax.ShapeDtypeStruct((DEC_BATCH, WINDOW, 128), F32),
                 jax.ShapeDtypeStruct((DEC_BATCH, WINDOW, 128), F32)]
    out_specs = [pl.BlockSpec((r, D_MODEL), lambda i: (i, 0)),
                 pl.BlockSpec((s, RET_HEADS, RET_QK, RET_V), lambda i: (i, 0, 0, 0)),
                 pl.BlockSpec((s, HG_HEADS, HG_K, HG_V), lambda i: (i, 0, 0, 0)),
                 pl.BlockSpec((s, WINDOW, 128), lambda i: (i, 0, 0)),
                 pl.BlockSpec((s, WINDOW, 128), lambda i: (i, 0, 0))]
    return pl.pallas_call(
        functools.partial(_sample_mix_kernel, layer=layer, gam4=gam4),
        grid=(DEC_BATCH // s,), in_specs=in_specs, out_specs=out_specs, out_shape=out_shape,
        scratch_shapes=[pltpu.VMEM((r, 512), F32)], name=f"sample_mix_{layer}",
        compiler_params=pltpu.CompilerParams(dimension_semantics=("arbitrary",),
                                             vmem_limit_bytes=VMEM_LIMIT),
    )(sinks, x, w_in, w_out, lbraw, hgw, g1, b1, *tabs, state_ret, state_hgrn, cache_k, cache_v,
      *consts)


def _decay_levels(rows, seqlen, levels):
    t = np.arange(rows)[:, None]
    u = np.arange(rows)[None, :]
    same = (t // seqlen) == (u // seqlen)
    blocks = [same & (u <= t)]
    for l in range(levels):
        m = 1 << l
        r = (t // (2 * m)) * (2 * m) + m
        lower = t >= r
        blocks.append(np.where(lower, (u >= r) & (u <= t), (u > t) & (u <= r - 1)))
    blocks.append(same & (u > t))
    return jnp.asarray(np.concatenate(blocks, axis=0).astype(np.float32), dtype=BF16)


def _rope_tables(pos):
    pos = pos.astype(F32)[:, None]
    inv = RET_THETA ** (-jnp.linspace(0.0, 1.0, RET_QK // 2, dtype=F32))
    ang = pos * inv[None, :]
    cr = jnp.tile(jnp.concatenate([jnp.cos(ang), jnp.cos(ang)], -1), (1, RET_HEADS))
    sr = jnp.tile(jnp.concatenate([-jnp.sin(ang), jnp.sin(ang)], -1), (1, RET_HEADS))
    inv = ROPE_THETA ** (-jnp.arange(0, ROT_DIM, 2, dtype=F32) / ROT_DIM)
    ang = pos * inv[None, :]
    rest = SWA_HD - ROT_DIM
    ones = jnp.ones((ang.shape[0], rest), F32)
    zeros = jnp.zeros((ang.shape[0], rest), F32)
    cs = jnp.tile(jnp.concatenate([jnp.cos(ang), jnp.cos(ang), ones], -1), (1, 2))
    ss = jnp.tile(jnp.concatenate([-jnp.sin(ang), jnp.sin(ang), zeros], -1), (1, 2))
    return cr, sr, cs, ss


def _ret_tables(tok, same):
    lg = jnp.log(1.0 - 2.0 ** (-5.0 - jnp.arange(RET_HEADS, dtype=F32)))
    n_tok = int(tok.max()) + 1
    tokf = jnp.asarray(tok, F32)
    diff = tokf[:, None] - tokf[None, :]
    dec = jnp.where(jnp.asarray(same) & (diff >= 0),
                    jnp.exp(jnp.maximum(diff, 0.0)[None] * lg[:, None, None]), 0.0)
    inter = jnp.repeat(jnp.exp((tokf + 1.0)[:, None] * lg[None, :]), RET_V, axis=1)
    kdec = jnp.repeat(jnp.exp((n_tok - 1.0 - tokf)[:, None] * lg[None, :]), RET_QK, axis=1)
    total = jnp.exp(n_tok * lg)
    return dec, inter, kdec, total


def kernel(x_prompt, x_sample, state_ret, state_hgrn, cache_swa_k, cache_swa_v, w_in, w_out,
           swa_sinks, hg_lb_raw, hg_norm_w, ln_mix_g, ln_mix_b, w_up, w_down, ln_ffn_g, ln_ffn_b):
    w_in_b, w_out_b = w_in.astype(BF16), w_out.astype(BF16)
    w_up_b, w_down_b = w_up.astype(BF16), w_down.astype(BF16)
    hgw = jnp.tile(hg_norm_w, (1, HG_HEADS)).reshape(DEPTH, 1, HG_HEADS * HG_V)
    g1, b1 = ln_mix_g.reshape(DEPTH, 1, D_MODEL), ln_mix_b.reshape(DEPTH, 1, D_MODEL)
    g2, b2 = ln_ffn_g.reshape(DEPTH, 1, D_MODEL), ln_ffn_b.reshape(DEPTH, 1, D_MODEL)
    cache_k = cache_swa_k.reshape(DEPTH, DEC_BATCH, WINDOW, 128)
    cache_v = cache_swa_v.reshape(DEPTH, DEC_BATCH, WINDOW, 128)

    hblk = jnp.asarray((np.arange(256)[:, None] // 64 == np.arange(256)[None, :] // 64)
                       .astype(np.float32), dtype=BF16)
    p_tabs = _rope_tables(jnp.arange(SEQ))
    tok = np.arange(CHUNK)
    dec, inter, kdec, total = _ret_tables(tok, np.ones((CHUNK, CHUNK), bool))
    rowdec = jnp.broadcast_to(jnp.repeat(total, RET_QK)[:, None], (128, 256))
    t_i, s_i = np.arange(CHUNK)[:, None], np.arange(CHUNK)[None, :]
    tri = jnp.asarray((s_i <= t_i).astype(np.float32), dtype=BF16)
    lvl = np.where(s_i < t_i, np.floor(np.log2(np.maximum(t_i ^ s_i, 1))),
                   np.where(s_i == t_i, PROMPT_LEVELS, -1)).astype(np.int32)
    q_i, k_j = np.arange(CHUNK)[:, None], np.arange(2 * CHUNK)[None, :]
    band = np.where((k_j >= q_i) & (k_j <= q_i + WINDOW), np.where(k_j >= WINDOW, 2, 1), 0)
    p_consts = (dec, inter, kdec, rowdec, tri, hblk, jnp.asarray(lvl),
                jnp.asarray(band.astype(np.int32)))
    row = np.arange(SAMPLE_ROWS)
    s_tabs = _rope_tables(PAST_LEN + jnp.asarray(row % DEC_SEQ))
    same = (row[:, None] // DEC_SEQ) == (row[None, :] // DEC_SEQ)
    dec_s, inter_s, kdec_s, total_s = _ret_tables(row % DEC_SEQ, same)
    s_consts = (dec_s, inter_s, kdec_s, _decay_levels(SAMPLE_ROWS, DEC_SEQ, SAMPLE_LEVELS), hblk)
    gam4 = tuple(float((1.0 - 2.0 ** (-5.0 - h)) ** DEC_SEQ) for h in range(RET_HEADS))
    del total_s

    xp = x_prompt
    xs = x_sample.reshape(DEC_BATCH * DEC_SEQ, D_MODEL)
    outs = [[] for _ in range(8)]
    for l in range(DEPTH):
        x1, st_r, st_h, pk, pv = _prompt_mix(l, xp, w_in_b, w_out_b, swa_sinks, hg_lb_raw, hgw,
                                             g1, b1, p_tabs, p_consts)
        xp = _ffn(l, x1.reshape(BATCH * SEQ, D_MODEL), w_up_b, w_down_b, g2, b2)
        xp = xp.reshape(BATCH, SEQ, D_MODEL)
        outs[0].append(jnp.stack([st_r[:, RET_QK * h:RET_QK * (h + 1), RET_V * h:RET_V * (h + 1)]
                                  for h in range(RET_HEADS)], axis=1))
        outs[1].append(jnp.stack([st_h[:, HG_K * h:HG_K * (h + 1), HG_V * h:HG_V * (h + 1)]
                                  for h in range(HG_HEADS)], axis=1))
        outs[2].append(pk.reshape(BATCH, SWA_KV, SWA_HD, WINDOW).transpose(0, 3, 1, 2))
        outs[3].append(pv.reshape(BATCH, SWA_KV, SWA_HD, WINDOW).transpose(0, 3, 1, 2))

        x1, ns_r, ns_h, nk, nv = _sample_mix(l, xs, w_in_b, w_out_b, swa_sinks, hg_lb_raw, hgw,
                                             g1, b1, s_tabs, state_ret, state_hgrn, cache_k,
                                             cache_v, s_consts, gam4)
        xs = _ffn(l, x1, w_up_b, w_down_b, g2, b2)
        outs[4].append(ns_r)
        outs[5].append(ns_h)
        outs[6].append(nk.reshape(DEC_BATCH, WINDOW, SWA_KV, SWA_HD))
        outs[7].append(nv.reshape(DEC_BATCH, WINDOW, SWA_KV, SWA_HD))

    return (xp, xs.reshape(DEC_BATCH, DEC_SEQ, D_MODEL)) + tuple(jnp.stack(o) for o in outs)
```

```python
import functools

import numpy as np
import jax
import jax.numpy as jnp
from jax import lax
from jax.experimental import pallas as pl
from jax.experimental.pallas import tpu as pltpu

F32 = jnp.float32
BF16 = jnp.bfloat16

D_MODEL = 1024
BATCH = 2
SEQ = 8192
DEPTH = 4
DEC_BATCH = 128
DEC_SEQ = 4
PAST_LEN = 8192
RET_HEADS = 4
RET_QK = 32
RET_V = 64
RET_THETA = 10000.0
HG_HEADS = 4
HG_K = 64
HG_V = 64
SWA_HEADS = 8
SWA_KV = 2
SWA_GROUP = SWA_HEADS // SWA_KV
SWA_HD = 64
WINDOW = 128
ROT_DIM = 16
ROPE_THETA = 500000.0
CHUNK = 128
D_FF = 4 * D_MODEL
IN_COLS = 2560
ALPHA = (2.0 * DEPTH) ** 0.25
LN_EPS = 1e-5
RMS_EPS = 1e-6
NEG = -1e30

C_RQ, C_RK, C_RV, C_RG = 0, 128, 256, 512
C_HQ, C_HF, C_HI, C_HG = 768, 1024, 1280, 1536
C_SQ, C_SK, C_SV = 1792, 2304, 2432

LANES = 128
PROMPT_TILE = 512
FFN_TILE = 512
SAMPLE_SEQS = 16
SAMPLE_ROWS = SAMPLE_SEQS * DEC_SEQ
PROMPT_LEVELS = 7
SAMPLE_LEVELS = 2
VMEM_LIMIT = 56 * 1024 * 1024


def _mm(a, b):
    return jnp.dot(a.astype(BF16), b.astype(BF16), preferred_element_type=F32)


def _mm_nt(a, b):
    return lax.dot_general(a.astype(BF16), b.astype(BF16), (((1,), (1,)), ((), ())),
                           preferred_element_type=F32)


def _iota(shape, dim):
    return lax.broadcasted_iota(jnp.int32, shape, dim)


def _layernorm(x, g, b):
    mu = jnp.mean(x, axis=-1, keepdims=True)
    xc = x - mu
    var = jnp.mean(xc * xc, axis=-1, keepdims=True)
    return xc * lax.rsqrt(var + LN_EPS) * g + b


def _rope(x, cos_t, sin_t, half, period):
    first_half = (_iota((1, LANES), 1) % period) < half
    fwd = pltpu.roll(x, LANES - half, 1)
    bwd = pltpu.roll(x, half, 1)
    return x * cos_t + jnp.where(first_half, fwd, bwd) * sin_t


def _silu(x):
    return x * (1.0 / (1.0 + jnp.exp(-x)))


def _split3(x):
    hi = x.astype(BF16)
    r = x - hi.astype(F32)
    mid = r.astype(BF16)
    lo = (r - mid.astype(F32)).astype(BF16)
    return hi, mid, lo


def _hg_lower_bound(lbraw_ref, layer):
    raw = lbraw_ref[...]
    e = jnp.exp(raw - jnp.max(raw, axis=0, keepdims=True))
    sm = e / jnp.sum(e, axis=0, keepdims=True)
    cum = sm[0:1]
    for i in range(1, layer + 1):
        cum = cum + sm[i:i + 1]
    return cum - sm[0:1]


def _hg_log_forget(hf, lb):
    logsig = jnp.minimum(hf, 0.0) - jnp.log(1.0 + jnp.exp(-jnp.abs(hf)))
    a = jnp.log(lb)
    b = jnp.log(1.0 - lb) + logsig
    return jnp.maximum(a, b) + jnp.log(1.0 + jnp.exp(-jnp.abs(a - b)))


def _level_exponents(b, rows, levels):
    g = rows // 8
    b3 = b.reshape(g, 8, 256)
    sub = _iota((g, 8, 256), 1)
    dn1 = pltpu.roll(b3, 1, 1)
    dn2 = pltpu.roll(b3, 2, 1)
    up1 = pltpu.roll(b3, 7, 1)
    out = [jnp.where(sub % 2 == 1, b3 - dn1, 0.0)]
    m4 = sub % 4
    out.append(jnp.where(m4 == 0, up1 - b3,
                         jnp.where(m4 == 1, 0.0, jnp.where(m4 == 2, b3 - dn1, b3 - dn2))))
    if levels > 2:
        r3 = jnp.broadcast_to(b3[:, 3:4, :], b3.shape)
        out.append(jnp.where(sub < 4, r3 - b3, b3 - r3))
        r7 = jnp.broadcast_to(b3[:, 7:8, :], b3.shape)
        grp = _iota((g, 8, 256), 0)
        for l in range(3, levels):
            span = (2 << l) // 8
            half = span // 2
            ref = r7.reshape(g // span, span, 8, 256)[:, half - 1:half]
            ref = jnp.broadcast_to(ref, (g // span, span, 8, 256)).reshape(g, 8, 256)
            out.append(jnp.where(grp % span >= half, b3 - ref, ref - b3))
    return [e.reshape(rows, 256) for e in out]


def _head_blockdiag(v, rows):
    r = _iota((4 * rows, 256), 0) // rows
    c = _iota((4 * rows, 256), 1) // 64
    return jnp.where(r == c, jnp.tile(v, (4, 1)), 0.0).astype(BF16)


def _rms_gate(o, gate, hblk_ref, w=None):
    ms = _mm(o * o, hblk_ref[...]) * (1.0 / 64.0)
    y = o * lax.rsqrt(ms + RMS_EPS)
    if w is not None:
        y = y * w
    return y * _silu(gate)


def _ret_intra(q, k, v, dec_ref, rows):
    lane_head = _iota((rows, 128), 1) // RET_QK
    a = [_mm_nt(jnp.where(lane_head == h, q, 0.0), k) * dec_ref[h] for h in range(RET_HEADS)]
    return _mm(jnp.concatenate(a, axis=1), _head_blockdiag(v, rows))


def _hgrn_intra(q, k, v, lf, mall_ref, hblk_ref, rows, levels):
    l3 = jnp.concatenate(_split3(lf), axis=1)
    eb = jnp.dot(mall_ref[...], l3, preferred_element_type=F32)
    eb = eb[:, 0:256] + eb[:, 256:512] + eb[:, 512:768]
    b = eb[0:rows]
    tail = eb[(levels + 1) * rows:(levels + 2) * rows]
    row = _iota((rows, 256), 0)
    lane_head = _iota((rows, 256), 1) // HG_K
    ra = _iota((rows, rows), 0)
    ca = _iota((rows, rows), 1)
    acc = [jnp.zeros((rows, rows), F32) for _ in range(HG_HEADS)]
    for l in range(levels):
        w = jnp.exp(eb[(l + 1) * rows:(l + 2) * rows])
        lower = ((row >> l) & 1) == 1
        qs = jnp.where(lower, q * w, 0.0).astype(BF16)
        ks = jnp.where(lower, 0.0, k * w).astype(BF16)
        same_block = (ra >> (l + 1)) == (ca >> (l + 1))
        for h in range(HG_HEADS):
            p = _mm_nt(jnp.where(lane_head == h, qs, jnp.zeros_like(qs)), ks)
            acc[h] = acc[h] + jnp.where(same_block, p, 0.0)
    o = _mm(jnp.concatenate(acc, axis=1), _head_blockdiag(v, rows))
    o = o + _mm(q * k, hblk_ref[...]) * v
    return o, b, tail


def _swa_heads(qcols, rope_q):
    out = []
    for c in range(4):
        qc = rope_q(qcols[:, LANES * c:LANES * (c + 1)]) * (SWA_HD ** -0.5)
        half = _iota(qc.shape, 1) // SWA_HD
        for p in range(2):
            g = (2 * c + p) // SWA_GROUP
            qm = jnp.where(half == p, qc, 0.0)
            if p != g:
                qm = pltpu.roll(qm, SWA_HD, 1)
            out.append(qm)
    return out


def _prompt_mix_kernel(sinks_ref, x_ref, xold_ref, win_ref, wout_ref, lbraw_ref, hgw_ref, g1_ref,
                       b1_ref, cr_ref, sr_ref, cs_ref, ss_ref, dec_ref, inter_ref, kdec_ref,
                       rowdec_ref, tri_ref, hblk_ref, lvl_ref, band_ref,
                       x1_ref, str_ref, sth_ref, pk_ref, pv_ref,
                       h_new, h_scr, cat_scr, cat_old, sret, shg, kprev, vprev, qs_scr, ks_scr,
                       ar_scr, ah_scr, *, layer):
    j = pl.program_id(0)
    seq_tiles = SEQ // PROMPT_TILE
    n_tiles = BATCH * seq_tiles
    n_chunks = PROMPT_TILE // CHUNK
    levels = PROMPT_LEVELS
    mix_tile = (j - 1) % seq_tiles

    @pl.when(mix_tile == 0)
    def _():
        sret[...] = jnp.zeros_like(sret)
        shg[...] = jnp.zeros_like(shg)
        kprev[...] = jnp.zeros_like(kprev)
        vprev[...] = jnp.zeros_like(vprev)

    n_parts = 5
    part_cols = IN_COLS // n_parts

    def project(rows, part):
        cols = slice(part * part_cols, (part + 1) * part_cols)
        h_new[rows, cols] = jnp.dot(x_ref[rows, :].astype(BF16), win_ref[:, cols],
                                    preferred_element_type=F32)

    def finish(rows):
        mix = jnp.dot(cat_old[rows, :], wout_ref[...], preferred_element_type=F32)
        x1_ref[rows, :] = _layernorm(ALPHA * xold_ref[rows, :] + mix, g1_ref[...], b1_ref[...])

    def mixers(c, rows, filler):
        lb = _hg_lower_bound(lbraw_ref, layer)
        cr, sr = cr_ref[rows, :], sr_ref[rows, :]
        cs, ss = cs_ref[rows, :], ss_ref[rows, :]

        q = _rope(h_scr[rows, C_RQ:C_RQ + 128], cr, sr, RET_QK // 2, RET_QK)
        k = _rope(h_scr[rows, C_RK:C_RK + 128], cr, sr, RET_QK // 2, RET_QK) * (RET_QK ** -0.5)
        v = h_scr[rows, C_RV:C_RV + 256]
        lane_head = _iota((CHUNK, 128), 1) // RET_QK
        for hd in range(RET_HEADS):
            a = _mm_nt(jnp.where(lane_head == hd, q, 0.0), k) * dec_ref[hd]
            ar_scr[:, CHUNK * hd:CHUNK * (hd + 1)] = a.astype(BF16)
        o_r = jnp.dot(ar_scr[...], _head_blockdiag(v, CHUNK), preferred_element_type=F32)
        s0 = sret[...]
        o_r = o_r + _mm(q, s0) * inter_ref[...]
        r_i = _iota((128, 256), 0) // RET_QK
        c_i = _iota((128, 256), 1) // RET_V
        upd = _mm((k * kdec_ref[...]).T, v)
        sret[...] = rowdec_ref[...] * s0 + jnp.where(r_i == c_i, upd, 0.0)
        ret = _rms_gate(o_r, h_scr[rows, C_RG:C_RG + 256], hblk_ref)
        filler(0)

        q = h_scr[rows, C_HQ:C_HQ + 256]
        lf = _hg_log_forget(h_scr[rows, C_HF:C_HF + 256], lb)
        k = 1.0 - jnp.exp(lf)
        v = h_scr[rows, C_HI:C_HI + 256]
        l3 = jnp.concatenate(_split3(lf), axis=1)
        b = jnp.dot(tri_ref[...], l3, preferred_element_type=F32)
        b = b[:, 0:256] + b[:, 256:512] + b[:, 512:768]
        for l, e in enumerate(_level_exponents(b, CHUNK, levels)):
            w = jnp.exp(e)
            qs_scr[l] = (q * w).astype(BF16)
            ks_scr[l] = (k * w).astype(BF16)
        qs_scr[levels] = q.astype(BF16)
        ks_scr[levels] = k.astype(BF16)
        filler(1)
        lvl = lvl_ref[...]
        for hd in range(HG_HEADS):
            lanes = slice(HG_K * hd, HG_K * (hd + 1))
            a = jnp.zeros((CHUNK, CHUNK), F32)
            for l in range(levels + 1):
                p = lax.dot_general(qs_scr[l, :, lanes], ks_scr[l, :, lanes],
                                    (((1,), (1,)), ((), ())), preferred_element_type=F32)
                a = jnp.where(lvl == l, p, a)
            ah_scr[:, CHUNK * hd:CHUNK * (hd + 1)] = a.astype(BF16)
        o_h = jnp.dot(ah_scr[...], _head_blockdiag(v, CHUNK), preferred_element_type=F32)
        s0 = shg[...]
        o_h = o_h + _mm(q * jnp.exp(b), s0)
        bl = b[CHUNK - 1:CHUNK, :]
        decay_t = jnp.broadcast_to(jnp.exp(bl), (128, 256)).T
        r_i = _iota((256, 256), 0) // HG_K
        c_i = _iota((256, 256), 1) // HG_V
        upd = _mm((k * jnp.exp(bl - b)).T, v)
        shg[...] = jnp.tile(decay_t, (1, 2)) * s0 + jnp.where(r_i == c_i, upd, 0.0)
        hg = _rms_gate(o_h, h_scr[rows, C_HG:C_HG + 256], hblk_ref, hgw_ref[...])
        filler(2)

        k = _rope(h_scr[rows, C_SK:C_SK + 128], cs, ss, ROT_DIM // 2, SWA_HD)
        v = h_scr[rows, C_SV:C_SV + 128]
        kk = jnp.concatenate([kprev[...], k.astype(BF16)], axis=0)
        vv = jnp.concatenate([vprev[...], v.astype(BF16)], axis=0)
        mask = band_ref[...] > jnp.where(mix_tile * n_chunks + c > 0, 0, 1)
        qm = _swa_heads(h_scr[rows, C_SQ:C_SQ + 512],
                        lambda t: _rope(t, cs, ss, ROT_DIM // 2, SWA_HD))
        outs = []
        for hd in range(SWA_HEADS):
            g = hd // SWA_GROUP
            sink = sinks_ref[layer, hd]
            s = jnp.where(mask, _mm_nt(qm[hd], kk), NEG)
            m = jnp.maximum(jnp.max(s, axis=1, keepdims=True), sink)
            p = jnp.exp(s - m)
            den = jnp.sum(p, axis=1, keepdims=True) + jnp.exp(sink - m)
            outs.append(_mm(p, vv)[:, SWA_HD * g:SWA_HD * (g + 1)] / den)
            if hd == SWA_HEADS // 2 - 1:
                filler(3)
        filler(4)
        kprev[...] = k.astype(BF16)
        vprev[...] = v.astype(BF16)
        pk_ref[...] = k.T
        pv_ref[...] = v.T

        cat_scr[rows, :] = jnp.concatenate([ret, hg] + outs, axis=1).astype(BF16)


    def run(do_project, do_mixers, do_finish):
        def body(c, carry):
            rows = pl.ds(pl.multiple_of(c * CHUNK, CHUNK), CHUNK)
            fill = (lambda part: project(rows, part)) if do_project else (lambda part: None)
            if do_mixers:
                mixers(c, rows, fill)
            elif do_project:
                for part in range(n_parts):
                    project(rows, part)
            if do_finish:
                finish(rows)
            if do_mixers:
                cat_old[rows, :] = cat_scr[rows, :]
            if do_project:
                h_scr[rows, :] = h_new[rows, :]
            return carry
        lax.fori_loop(0, n_chunks, body, 0)
        if do_mixers:
            str_ref[...] = sret[...]
            sth_ref[...] = shg[...]

    pl.when(j == 0)(lambda: run(True, False, False))
    pl.when(j == 1)(lambda: run(True, True, False))
    pl.when((j >= 2) & (j < n_tiles))(lambda: run(True, True, True))
    pl.when(j == n_tiles)(lambda: run(False, True, True))
    pl.when(j == n_tiles + 1)(lambda: run(False, False, True))


def _const_spec(shape, ngrid):
    zeros = (0,) * len(shape)
    if ngrid == 1:
        return pl.BlockSpec(shape, lambda i: zeros)
    return pl.BlockSpec(shape, lambda b, j: zeros)


def _prompt_mix(layer, x, w_in, w_out, sinks, lbraw, hgw, g1, b1, tabs, consts):
    t = PROMPT_TILE
    seq_tiles = SEQ // t
    wspec = lambda r, c: pl.BlockSpec((None, r, c), lambda j: (layer, 0, 0),
                                      pipeline_mode=pl.Buffered(1))
    vspec = lambda c: pl.BlockSpec((None, 1, c), lambda j: (layer, 0, 0))
    last = BATCH * seq_tiles - 1
    tile_new = lambda j: jnp.minimum(j, last)
    tile_mix = lambda j: jnp.clip(j - 1, 0, last)
    tile_old = lambda j: jnp.clip(j - 2, 0, last)
    tspec = pl.BlockSpec((t, LANES), lambda j: (tile_mix(j) % seq_tiles, 0))
    sspec = lambda r, c: pl.BlockSpec((None, r, c), lambda j: (tile_mix(j) // seq_tiles, 0, 0))
    in_specs = [pl.BlockSpec(memory_space=pltpu.SMEM),
                pl.BlockSpec((t, D_MODEL), lambda j: (tile_new(j), 0)),
                pl.BlockSpec((t, D_MODEL), lambda j: (tile_old(j), 0)),
                wspec(D_MODEL, IN_COLS), wspec(D_MODEL, D_MODEL),
                _const_spec((DEPTH, 256), 1), vspec(256), vspec(D_MODEL), vspec(D_MODEL),
                tspec, tspec, tspec, tspec] + [_const_spec(c.shape, 1) for c in consts]
    out_shape = [jax.ShapeDtypeStruct((BATCH * SEQ, D_MODEL), F32),
                 jax.ShapeDtypeStruct((BATCH, 128, 256), F32),
                 jax.ShapeDtypeStruct((BATCH, 256, 256), F32),
                 jax.ShapeDtypeStruct((BATCH, WINDOW, 128), F32),
                 jax.ShapeDtypeStruct((BATCH, WINDOW, 128), F32)]
    out_specs = [pl.BlockSpec((t, D_MODEL), lambda j: (tile_old(j), 0)),
                 sspec(128, 256), sspec(256, 256), sspec(WINDOW, 128), sspec(WINDOW, 128)]
    scratch = [pltpu.VMEM((t, IN_COLS), F32), pltpu.VMEM((t, IN_COLS), F32),
               pltpu.VMEM((t, D_MODEL), BF16), pltpu.VMEM((t, D_MODEL), BF16),
               pltpu.VMEM((128, 256), F32), pltpu.VMEM((256, 256), F32),
               pltpu.VMEM((WINDOW, 128), BF16), pltpu.VMEM((WINDOW, 128), BF16),
               pltpu.VMEM((PROMPT_LEVELS + 1, CHUNK, 256), BF16),
               pltpu.VMEM((PROMPT_LEVELS + 1, CHUNK, 256), BF16),
               pltpu.VMEM((CHUNK, 4 * CHUNK), BF16), pltpu.VMEM((CHUNK, 4 * CHUNK), BF16)]
    return pl.pallas_call(
        functools.partial(_prompt_mix_kernel, layer=layer),
        grid=(BATCH * seq_tiles + 2,), in_specs=in_specs, out_specs=out_specs,
        out_shape=out_shape, scratch_shapes=scratch, name=f"prompt_mix_{layer}",
        compiler_params=pltpu.CompilerParams(dimension_semantics=("arbitrary",),
                                             vmem_limit_bytes=VMEM_LIMIT),
    )(sinks, x, x, w_in, w_out, lbraw, hgw, g1, b1, *tabs, *consts)


def _ffn_kernel(x_ref, wup_ref, wdn_ref, g_ref, b_ref, o_ref):
    x = x_ref[...]
    xb = x.astype(BF16)
    acc = jnp.zeros(x.shape, F32)
    step = 1024
    for i in range(D_FF // step):
        hid = jnp.maximum(jnp.dot(xb, wup_ref[:, i * step:(i + 1) * step],
                                  preferred_element_type=F32), 0.0)
        acc = acc + jnp.dot((hid * hid).astype(BF16), wdn_ref[i * step:(i + 1) * step, :],
                            preferred_element_type=F32)
    o_ref[...] = _layernorm(ALPHA * x + acc, g_ref[...], b_ref[...])


def _ffn(layer, x, w_up, w_down, g2, b2):
    rows = x.shape[0]
    t = FFN_TILE
    return pl.pallas_call(
        _ffn_kernel,
        grid=(rows // t,),
        in_specs=[pl.BlockSpec((t, D_MODEL), lambda i: (i, 0)),
                  pl.BlockSpec((None, D_MODEL, D_FF), lambda i: (layer, 0, 0),
                               pipeline_mode=pl.Buffered(1)),
                  pl.BlockSpec((None, D_FF, D_MODEL), lambda i: (layer, 0, 0),
                               pipeline_mode=pl.Buffered(1)),
                  pl.BlockSpec((None, 1, D_MODEL), lambda i: (layer, 0, 0)),
                  pl.BlockSpec((None, 1, D_MODEL), lambda i: (layer, 0, 0))],
        out_specs=pl.BlockSpec((t, D_MODEL), lambda i: (i, 0)),
        out_shape=jax.ShapeDtypeStruct((rows, D_MODEL), F32),
        name=f"ffn_{layer}_{rows}",
        compiler_params=pltpu.CompilerParams(dimension_semantics=("arbitrary",),
                                             vmem_limit_bytes=VMEM_LIMIT),
    )(x, w_up, w_down, g2, b2)


def _sample_mix_kernel(sinks_ref, x_ref, win_ref, wout_ref, lbraw_ref, hgw_ref, g1_ref, b1_ref,
                       cr_ref, sr_ref, cs_ref, ss_ref, sret_ref, shg_ref, ck_ref, cv_ref,
                       dec_ref, inter_ref, kdec_ref, mall_ref, hblk_ref,
                       x1_ref, nsret_ref, nshg_ref, nk_ref, nv_ref, os_scr, *, layer, gam4):
    rows, seqs = SAMPLE_ROWS, SAMPLE_SEQS
    x = x_ref[...]
    h = jnp.dot(x.astype(BF16), win_ref[...], preferred_element_type=F32)
    cr, sr, cs, ss = cr_ref[...], sr_ref[...], cs_ref[...], ss_ref[...]
    lb = _hg_lower_bound(lbraw_ref, layer)

    q = _rope(h[:, C_RQ:C_RQ + 128], cr, sr, RET_QK // 2, RET_QK)
    k = _rope(h[:, C_RK:C_RK + 128], cr, sr, RET_QK // 2, RET_QK) * (RET_QK ** -0.5)
    v = h[:, C_RV:C_RV + 256]
    o_r = _ret_intra(q, k, v, dec_ref, rows)
    hk = RET_HEADS * RET_QK
    s0 = sret_ref[...].reshape(seqs * hk, RET_V)
    r_i = (_iota((seqs * hk, 256), 0) % hk) // RET_QK
    c_i = _iota((seqs * hk, 256), 1) // RET_V
    s0bd = jnp.where(r_i == c_i, jnp.tile(s0, (1, 4)), 0.0).astype(BF16)
    own = _iota((rows, seqs * hk), 0) // DEC_SEQ == _iota((rows, seqs * hk), 1) // hk
    o_r = o_r + _mm(jnp.where(own, jnp.tile(q, (1, seqs)), 0.0), s0bd) * inter_ref[...]
    kdt = (k * kdec_ref[...]).T
    col_seq = _iota((hk, rows), 1) // DEC_SEQ
    for n in range(seqs):
        upd = _mm(jnp.where(col_seq == n, kdt, 0.0), v)
        for hd in range(RET_HEADS):
            nsret_ref[n, hd] = (gam4[hd] * sret_ref[n, hd]
                                + upd[RET_QK * hd:RET_QK * (hd + 1), RET_V * hd:RET_V * (hd + 1)])
    ret = _rms_gate(o_r, h[:, C_RG:C_RG + 256], hblk_ref)

    q = h[:, C_HQ:C_HQ + 256]
    lf = _hg_log_forget(h[:, C_HF:C_HF + 256], lb)
    k = 1.0 - jnp.exp(lf)
    v = h[:, C_HI:C_HI + 256]
    o_h, b, tail = _hgrn_intra(q, k, v, lf, mall_ref, hblk_ref, rows, SAMPLE_LEVELS)
    hk = HG_HEADS * HG_K
    s0 = shg_ref[...].reshape(seqs * hk, HG_V)
    r_i = (_iota((seqs * hk, 256), 0) % hk) // HG_K
    c_i = _iota((seqs * hk, 256), 1) // HG_V
    s0bd = jnp.where(r_i == c_i, jnp.tile(s0, (1, 4)), 0.0).astype(BF16)
    own = _iota((rows, seqs * hk), 0) // DEC_SEQ == _iota((rows, seqs * hk), 1) // hk
    o_h = o_h + _mm(jnp.where(own, jnp.tile(q * jnp.exp(b), (1, seqs)), 0.0), s0bd)
    kdt = (k * jnp.exp(tail)).T
    ebt = jnp.exp(b).T
    col_seq = _iota((hk, rows), 1) // DEC_SEQ
    for n in range(seqs):
        upd = _mm(jnp.where(col_seq == n, kdt, 0.0), v)
        last = DEC_SEQ * n + DEC_SEQ - 1
        dcol = ebt[:, last:last + 1]
        for hd in range(HG_HEADS):
            nshg_ref[n, hd] = (dcol[HG_K * hd:HG_K * (hd + 1)] * shg_ref[n, hd]
                               + upd[HG_K * hd:HG_K * (hd + 1), HG_V * hd:HG_V * (hd + 1)])
    hg = _rms_gate(o_h, h[:, C_HG:C_HG + 256], hblk_ref, hgw_ref[...])

    k = _rope(h[:, C_SK:C_SK + 128], cs, ss, ROT_DIM // 2, SWA_HD)
    v = h[:, C_SV:C_SV + 128]
    kb, vb = k.astype(BF16), v.astype(BF16)
    qm = _swa_heads(h[:, C_SQ:C_SQ + 512], lambda t: _rope(t, cs, ss, ROT_DIM // 2, SWA_HD))
    r64 = _iota((64, 1), 0)
    sinkcol = jnp.zeros((64, 1), F32)
    for hd in range(SWA_HEADS):
        sinkcol = jnp.where(r64 // 8 == hd, sinks_ref[layer, hd], sinkcol)
    rc = _iota((64, 128), 0)
    cc = _iota((64, 128), 1)
    first_seq = (rc % 8) < DEC_SEQ
    cmask = cc >= rc % DEC_SEQ
    rn = _iota((64, rows), 0)
    cn = _iota((64, rows), 1)
    sub = _iota((8, 128), 0)
    for pr in range(seqs // 2):
        na, nb = 2 * pr, 2 * pr + 1
        qp = jnp.concatenate([t[8 * pr:8 * pr + 8] for t in qm], axis=0)
        ka, kb_c = ck_ref[na], ck_ref[nb]
        va, vb_c = cv_ref[na], cv_ref[nb]
        sc = jnp.where(first_seq, _mm_nt(qp, ka), _mm_nt(qp, kb_c))
        sc = jnp.where(cmask, sc, NEG)
        nmask = (cn // DEC_SEQ == 2 * pr + (rn % 8) // DEC_SEQ) & (cn % DEC_SEQ <= rn % DEC_SEQ)
        sn = jnp.where(nmask, _mm_nt(qp, kb), NEG)
        m = jnp.maximum(jnp.maximum(jnp.max(sc, axis=1, keepdims=True),
                                    jnp.max(sn, axis=1, keepdims=True)), sinkcol)
        pc = jnp.exp(sc - m)
        pn = jnp.exp(sn - m)
        den = (jnp.sum(pc, axis=1, keepdims=True) + jnp.sum(pn, axis=1, keepdims=True)
               + jnp.exp(sinkcol - m))
        o = (_mm(jnp.where(first_seq, pc, 0.0), va) + _mm(jnp.where(first_seq, 0.0, pc), vb_c)
             + _mm(pn, vb)) / den
        pieces = []
        for hd in range(SWA_HEADS):
            g = hd // SWA_GROUP
            pieces.append(o[8 * hd:8 * hd + 8, SWA_HD * g:SWA_HD * (g + 1)])
        os_scr[8 * pr:8 * pr + 8, :] = jnp.concatenate(pieces, axis=1)
        for new, old_a, old_b, dst in ((k[8 * pr:8 * pr + 8], ka, kb_c, nk_ref),
                                       (v[8 * pr:8 * pr + 8], va, vb_c, nv_ref)):
            ra = pltpu.roll(old_a, WINDOW - DEC_SEQ, 0)
            rb = pltpu.roll(old_b, WINDOW - DEC_SEQ, 0)
            dst[na, 0:120] = ra[0:120]
            dst[na, 120:128] = jnp.where(sub >= DEC_SEQ, pltpu.roll(new, DEC_SEQ, 0), ra[120:128])
            dst[nb, 0:120] = rb[0:120]
            dst[nb, 120:128] = jnp.where(sub >= DEC_SEQ, new, rb[120:128])

    cat = jnp.concatenate([ret, hg, os_scr[...]], axis=1).astype(BF16)
    mix = jnp.dot(cat, wout_ref[...], preferred_element_type=F32)
    x1_ref[...] = _layernorm(ALPHA * x + mix, g1_ref[...], b1_ref[...])


def _sample_mix(layer, x, w_in, w_out, sinks, lbraw, hgw, g1, b1, tabs, state_ret, state_hgrn,
                cache_k, cache_v, consts, gam4):
    r, s = SAMPLE_ROWS, SAMPLE_SEQS
    wspec = lambda a, c: pl.BlockSpec((None, a, c), lambda i: (layer, 0, 0),
                                      pipeline_mode=pl.Buffered(1))
    vspec = lambda c: pl.BlockSpec((None, 1, c), lambda i: (layer, 0, 0))
    in_specs = [pl.BlockSpec(memory_space=pltpu.SMEM),
                pl.BlockSpec((r, D_MODEL), lambda i: (i, 0)),
                wspec(D_MODEL, IN_COLS), wspec(D_MODEL, D_MODEL),
                _const_spec((DEPTH, 256), 1), vspec(256), vspec(D_MODEL), vspec(D_MODEL)]
    in_specs += [_const_spec((r, LANES), 1)] * 4
    in_specs += [pl.BlockSpec((None, s, RET_HEADS, RET_QK, RET_V), lambda i: (layer, i, 0, 0, 0)),
                 pl.BlockSpec((None, s, HG_HEADS, HG_K, HG_V), lambda i: (layer, i, 0, 0, 0)),
                 pl.BlockSpec((None, s, WINDOW, 128), lambda i: (layer, i, 0, 0)),
                 pl.BlockSpec((None, s, WINDOW, 128), lambda i: (layer, i, 0, 0))]
    in_specs += [_const_spec(c.shape, 1) for c in consts]
    out_shape = [jax.ShapeDtypeStruct((DEC_BATCH * DEC_SEQ, D_MODEL), F32),
                 jax.ShapeDtypeStruct((DEC_BATCH, RET_HEADS, RET_QK, RET_V), F32),
                 jax.ShapeDtypeStruct((DEC_BATCH, HG_HEADS, HG_K, HG_V), F32),
                 jax.ShapeDtypeStruct((DEC_BATCH, WINDOW, 128), F32),
                 jax.ShapeDtypeStruct((DEC_BATCH, WINDOW, 128), F32)]
    out_specs = [pl.BlockSpec((r, D_MODEL), lambda i: (i, 0)),
                 pl.BlockSpec((s, RET_HEADS, RET_QK, RET_V), lambda i: (i, 0, 0, 0)),
                 pl.BlockSpec((s, HG_HEADS, HG_K, HG_V), lambda i: (i, 0, 0, 0)),
                 pl.BlockSpec((s, WINDOW, 128), lambda i: (i, 0, 0)),
                 pl.BlockSpec((s, WINDOW, 128), lambda i: (i, 0, 0))]
    return pl.pallas_call(
        functools.partial(_sample_mix_kernel, layer=layer, gam4=gam4),
        grid=(DEC_BATCH // s,), in_specs=in_specs, out_specs=out_specs, out_shape=out_shape,
        scratch_shapes=[pltpu.VMEM((r, 512), F32)], name=f"sample_mix_{layer}",
        compiler_params=pltpu.CompilerParams(dimension_semantics=("arbitrary",),
                                             vmem_limit_bytes=VMEM_LIMIT),
    )(sinks, x, w_in, w_out, lbraw, hgw, g1, b1, *tabs, state_ret, state_hgrn, cache_k, cache_v,
      *consts)


def _decay_levels(rows, seqlen, levels):
    t = np.arange(rows)[:, None]
    u = np.arange(rows)[None, :]
    same = (t // seqlen) == (u // seqlen)
    blocks = [same & (u <= t)]
    for l in range(levels):
        m = 1 << l
        r = (t // (2 * m)) * (2 * m) + m
        lower = t >= r
        blocks.append(np.where(lower, (u >= r) & (u <= t), (u > t) & (u <= r - 1)))
    blocks.append(same & (u > t))
    return jnp.asarray(np.concatenate(blocks, axis=0).astype(np.float32), dtype=BF16)


def _rope_tables(pos):
    pos = pos.astype(F32)[:, None]
    inv = RET_THETA ** (-jnp.linspace(0.0, 1.0, RET_QK // 2, dtype=F32))
    ang = pos * inv[None, :]
    cr = jnp.tile(jnp.concatenate([jnp.cos(ang), jnp.cos(ang)], -1), (1, RET_HEADS))
    sr = jnp.tile(jnp.concatenate([-jnp.sin(ang), jnp.sin(ang)], -1), (1, RET_HEADS))
    inv = ROPE_THETA ** (-jnp.arange(0, ROT_DIM, 2, dtype=F32) / ROT_DIM)
    ang = pos * inv[None, :]
    rest = SWA_HD - ROT_DIM
    ones = jnp.ones((ang.shape[0], rest), F32)
    zeros = jnp.zeros((ang.shape[0], rest), F32)
    cs = jnp.tile(jnp.concatenate([jnp.cos(ang), jnp.cos(ang), ones], -1), (1, 2))
    ss = jnp.tile(jnp.concatenate([-jnp.sin(ang), jnp.sin(ang), zeros], -1), (1, 2))
    return cr, sr, cs, ss


def _ret_tables(tok, same):
    lg = jnp.log(1.0 - 2.0 ** (-5.0 - jnp.arange(RET_HEADS, dtype=F32)))
    n_tok = int(tok.max()) + 1
    tokf = jnp.asarray(tok, F32)
    diff = tokf[:, None] - tokf[None, :]
    dec = jnp.where(jnp.asarray(same) & (diff >= 0),
                    jnp.exp(jnp.maximum(diff, 0.0)[None] * lg[:, None, None]), 0.0)
    inter = jnp.repeat(jnp.exp((tokf + 1.0)[:, None] * lg[None, :]), RET_V, axis=1)
    kdec = jnp.repeat(jnp.exp((n_tok - 1.0 - tokf)[:, None] * lg[None, :]), RET_QK, axis=1)
    total = jnp.exp(n_tok * lg)
    return dec, inter, kdec, total


def kernel(x_prompt, x_sample, state_ret, state_hgrn, cache_swa_k, cache_swa_v, w_in, w_out,
           swa_sinks, hg_lb_raw, hg_norm_w, ln_mix_g, ln_mix_b, w_up, w_down, ln_ffn_g, ln_ffn_b):
    w_in_b, w_out_b = w_in.astype(BF16), w_out.astype(BF16)
    w_up_b, w_down_b = w_up.astype(BF16), w_down.astype(BF16)
    hgw = jnp.tile(hg_norm_w, (1, HG_HEADS)).reshape(DEPTH, 1, HG_HEADS * HG_V)
    g1, b1 = ln_mix_g.reshape(DEPTH, 1, D_MODEL), ln_mix_b.reshape(DEPTH, 1, D_MODEL)
    g2, b2 = ln_ffn_g.reshape(DEPTH, 1, D_MODEL), ln_ffn_b.reshape(DEPTH, 1, D_MODEL)
    cache_k = cache_swa_k.reshape(DEPTH, DEC_BATCH, WINDOW, 128)
    cache_v = cache_swa_v.reshape(DEPTH, DEC_BATCH, WINDOW, 128)

    hblk = jnp.asarray((np.arange(256)[:, None] // 64 == np.arange(256)[None, :] // 64)
                       .astype(np.float32), dtype=BF16)
    p_tabs = _rope_tables(jnp.arange(SEQ))
    tok = np.arange(CHUNK)
    dec, inter, kdec, total = _ret_tables(tok, np.ones((CHUNK, CHUNK), bool))
    rowdec = jnp.broadcast_to(jnp.repeat(total, RET_QK)[:, None], (128, 256))
    t_i, s_i = np.arange(CHUNK)[:, None], np.arange(CHUNK)[None, :]
    tri = jnp.asarray((s_i <= t_i).astype(np.float32), dtype=BF16)
    lvl = np.where(s_i < t_i, np.floor(np.log2(np.maximum(t_i ^ s_i, 1))),
                   np.where(s_i == t_i, PROMPT_LEVELS, -1)).astype(np.int32)
    q_i, k_j = np.arange(CHUNK)[:, None], np.arange(2 * CHUNK)[None, :]
    band = np.where((k_j >= q_i) & (k_j <= q_i + WINDOW), np.where(k_j >= WINDOW, 2, 1), 0)
    p_consts = (dec, inter, kdec, rowdec, tri, hblk, jnp.asarray(lvl),
                jnp.asarray(band.astype(np.int32)))
    row = np.arange(SAMPLE_ROWS)
    s_tabs = _rope_tables(PAST_LEN + jnp.asarray(row % DEC_SEQ))
    same = (row[:, None] // DEC_SEQ) == (row[None, :] // DEC_SEQ)
    dec_s, inter_s, kdec_s, total_s = _ret_tables(row % DEC_SEQ, same)
    s_consts = (dec_s, inter_s, kdec_s, _decay_levels(SAMPLE_ROWS, DEC_SEQ, SAMPLE_LEVELS), hblk)
    gam4 = tuple(float((1.0 - 2.0 ** (-5.0 - h)) ** DEC_SEQ) for h in range(RET_HEADS))
    del total_s

    xp = x_prompt.reshape(BATCH * SEQ, D_MODEL)
    xs = x_sample.reshape(DEC_BATCH * DEC_SEQ, D_MODEL)
    outs = [[] for _ in range(8)]
    for l in range(DEPTH):
        x1, st_r, st_h, pk, pv = _prompt_mix(l, xp, w_in_b, w_out_b, swa_sinks, hg_lb_raw, hgw,
                                             g1, b1, p_tabs, p_consts)
        xp = _ffn(l, x1, w_up_b, w_down_b, g2, b2)
        outs[0].append(jnp.stack([st_r[:, RET_QK * h:RET_QK * (h + 1), RET_V * h:RET_V * (h + 1)]
                                  for h in range(RET_HEADS)], axis=1))
        outs[1].append(jnp.stack([st_h[:, HG_K * h:HG_K * (h + 1), HG_V * h:HG_V * (h + 1)]
                                  for h in range(HG_HEADS)], axis=1))
        outs[2].append(pk.reshape(BATCH, SWA_KV, SWA_HD, WINDOW).transpose(0, 3, 1, 2))
        outs[3].append(pv.reshape(BATCH, SWA_KV, SWA_HD, WINDOW).transpose(0, 3, 1, 2))

        x1, ns_r, ns_h, nk, nv = _sample_mix(l, xs, w_in_b, w_out_b, swa_sinks, hg_lb_raw, hgw,
                                             g1, b1, s_tabs, state_ret, state_hgrn, cache_k,
                                             cache_v, s_consts, gam4)
        xs = _ffn(l, x1, w_up_b, w_down_b, g2, b2)
        outs[4].append(ns_r)
        outs[5].append(ns_h)
        outs[6].append(nk.reshape(DEC_BATCH, WINDOW, SWA_KV, SWA_HD))
        outs[7].append(nv.reshape(DEC_BATCH, WINDOW, SWA_KV, SWA_HD))

    return ((xp.reshape(BATCH, SEQ, D_MODEL), xs.reshape(DEC_BATCH, DEC_SEQ, D_MODEL))
            + tuple(jnp.stack(o) for o in outs))
```

```python
import functools

import numpy as np
import jax
import jax.numpy as jnp
from jax import lax
from jax.experimental import pallas as pl
from jax.experimental.pallas import tpu as pltpu

F32 = jnp.float32
BF16 = jnp.bfloat16

D_MODEL = 1024
BATCH = 2
SEQ = 8192
DEPTH = 4
DEC_BATCH = 128
DEC_SEQ = 4
PAST_LEN = 8192
RET_HEADS = 4
RET_QK = 32
RET_V = 64
RET_THETA = 10000.0
HG_HEADS = 4
HG_K = 64
HG_V = 64
SWA_HEADS = 8
SWA_KV = 2
SWA_GROUP = SWA_HEADS // SWA_KV
SWA_HD = 64
WINDOW = 128
ROT_DIM = 16
ROPE_THETA = 500000.0
CHUNK = 128
D_FF = 4 * D_MODEL
IN_COLS = 2560
ALPHA = (2.0 * DEPTH) ** 0.25
LN_EPS = 1e-5
RMS_EPS = 1e-6
NEG = -1e30

C_RQ, C_RK, C_RV, C_RG = 0, 128, 256, 512
C_HQ, C_HF, C_HI, C_HG = 768, 1024, 1280, 1536
C_SQ, C_SK, C_SV = 1792, 2304, 2432

LANES = 128
PROMPT_TILE = 512
FFN_TILE = 512
SAMPLE_SEQS = 16
SAMPLE_ROWS = SAMPLE_SEQS * DEC_SEQ
PROMPT_LEVELS = 7
SAMPLE_LEVELS = 2
VMEM_LIMIT = 56 * 1024 * 1024


def _mm(a, b):
    return jnp.dot(a.astype(BF16), b.astype(BF16), preferred_element_type=F32)


def _mm_nt(a, b):
    return lax.dot_general(a.astype(BF16), b.astype(BF16), (((1,), (1,)), ((), ())),
                           preferred_element_type=F32)


def _iota(shape, dim):
    return lax.broadcasted_iota(jnp.int32, shape, dim)


def _layernorm(x, g, b):
    mu = jnp.mean(x, axis=-1, keepdims=True)
    xc = x - mu
    var = jnp.mean(xc * xc, axis=-1, keepdims=True)
    return xc * lax.rsqrt(var + LN_EPS) * g + b


def _rope(x, cos_t, sin_t, half, period):
    first_half = (_iota((1, LANES), 1) % period) < half
    fwd = pltpu.roll(x, LANES - half, 1)
    bwd = pltpu.roll(x, half, 1)
    return x * cos_t + jnp.where(first_half, fwd, bwd) * sin_t


def _silu(x):
    return x * (1.0 / (1.0 + jnp.exp(-x)))


def _split3(x):
    hi = x.astype(BF16)
    r = x - hi.astype(F32)
    mid = r.astype(BF16)
    lo = (r - mid.astype(F32)).astype(BF16)
    return hi, mid, lo


def _hg_lower_bound(lbraw_ref, layer):
    raw = lbraw_ref[...]
    e = jnp.exp(raw - jnp.max(raw, axis=0, keepdims=True))
    sm = e / jnp.sum(e, axis=0, keepdims=True)
    cum = sm[0:1]
    for i in range(1, layer + 1):
        cum = cum + sm[i:i + 1]
    return cum - sm[0:1]


def _hg_log_forget(hf, lb):
    logsig = jnp.minimum(hf, 0.0) - jnp.log(1.0 + jnp.exp(-jnp.abs(hf)))
    a = jnp.log(lb)
    b = jnp.log(1.0 - lb) + logsig
    return jnp.maximum(a, b) + jnp.log(1.0 + jnp.exp(-jnp.abs(a - b)))


def _level_exponents(b, rows, levels):
    g = rows // 8
    n = b.shape[1]
    b3 = b.reshape(g, 8, n)
    sub = _iota((g, 8, n), 1)
    dn1 = pltpu.roll(b3, 1, 1)
    dn2 = pltpu.roll(b3, 2, 1)
    up1 = pltpu.roll(b3, 7, 1)
    out = [jnp.where(sub % 2 == 1, b3 - dn1, 0.0)]
    m4 = sub % 4
    out.append(jnp.where(m4 == 0, up1 - b3,
                         jnp.where(m4 == 1, 0.0, jnp.where(m4 == 2, b3 - dn1, b3 - dn2))))
    if levels > 2:
        r3 = jnp.broadcast_to(b3[:, 3:4, :], b3.shape)
        out.append(jnp.where(sub < 4, r3 - b3, b3 - r3))
        r7 = jnp.broadcast_to(b3[:, 7:8, :], b3.shape)
        grp = _iota((g, 8, n), 0)
        for l in range(3, levels):
            span = (2 << l) // 8
            half = span // 2
            ref = r7.reshape(g // span, span, 8, n)[:, half - 1:half]
            ref = jnp.broadcast_to(ref, (g // span, span, 8, n)).reshape(g, 8, n)
            out.append(jnp.where(grp % span >= half, b3 - ref, ref - b3))
    return [e.reshape(rows, n) for e in out]


def _head_blockdiag(v, rows):
    r = _iota((4 * rows, 256), 0) // rows
    c = _iota((4 * rows, 256), 1) // 64
    return jnp.where(r == c, jnp.tile(v, (4, 1)), 0.0).astype(BF16)


def _rms_gate(o, gate, hblk_ref, w=None):
    ms = _mm(o * o, hblk_ref[...]) * (1.0 / 64.0)
    y = o * lax.rsqrt(ms + RMS_EPS)
    if w is not None:
        y = y * w
    return y * _silu(gate)


def _ret_intra(q, k, v, dec_ref, rows):
    lane_head = _iota((rows, 128), 1) // RET_QK
    a = [_mm_nt(jnp.where(lane_head == h, q, 0.0), k) * dec_ref[h] for h in range(RET_HEADS)]
    return _mm(jnp.concatenate(a, axis=1), _head_blockdiag(v, rows))


def _hgrn_intra(q, k, v, lf, mall_ref, hblk_ref, rows, levels):
    l3 = jnp.concatenate(_split3(lf), axis=1)
    eb = jnp.dot(mall_ref[...], l3, preferred_element_type=F32)
    eb = eb[:, 0:256] + eb[:, 256:512] + eb[:, 512:768]
    b = eb[0:rows]
    tail = eb[(levels + 1) * rows:(levels + 2) * rows]
    row = _iota((rows, 256), 0)
    lane_head = _iota((rows, 256), 1) // HG_K
    ra = _iota((rows, rows), 0)
    ca = _iota((rows, rows), 1)
    acc = [jnp.zeros((rows, rows), F32) for _ in range(HG_HEADS)]
    for l in range(levels):
        w = jnp.exp(eb[(l + 1) * rows:(l + 2) * rows])
        lower = ((row >> l) & 1) == 1
        qs = jnp.where(lower, q * w, 0.0).astype(BF16)
        ks = jnp.where(lower, 0.0, k * w).astype(BF16)
        same_block = (ra >> (l + 1)) == (ca >> (l + 1))
        for h in range(HG_HEADS):
            p = _mm_nt(jnp.where(lane_head == h, qs, jnp.zeros_like(qs)), ks)
            acc[h] = acc[h] + jnp.where(same_block, p, 0.0)
    o = _mm(jnp.concatenate(acc, axis=1), _head_blockdiag(v, rows))
    o = o + _mm(q * k, hblk_ref[...]) * v
    return o, b, tail


def _swa_heads(qcols, rope_q):
    out = []
    for c in range(4):
        qc = rope_q(qcols[:, LANES * c:LANES * (c + 1)]) * (SWA_HD ** -0.5)
        half = _iota(qc.shape, 1) // SWA_HD
        for p in range(2):
            g = (2 * c + p) // SWA_GROUP
            qm = jnp.where(half == p, qc, 0.0)
            if p != g:
                qm = pltpu.roll(qm, SWA_HD, 1)
            out.append(qm)
    return out


def _prompt_mix_kernel(sinks_ref, x_ref, xold_ref, win_ref, wout_ref, lbraw_ref, hgw_ref, g1_ref,
                       b1_ref, cr_ref, sr_ref, cs_ref, ss_ref, dec_ref, inter_ref, kdec_ref,
                       rowdec_ref, tri_ref, hblk_ref, lvl_ref, band_ref,
                       x1_ref, str_ref, sth_ref, pk_ref, pv_ref,
                       h_new, h_scr, cat_scr, cat_old, sret, shg, kprev, vprev, qs_scr, ks_scr,
                       ar_scr, ah_scr, qb_scr, kb_scr, bl_scr, *, layer):
    j = pl.program_id(0)
    seq_tiles = SEQ // PROMPT_TILE
    n_tiles = BATCH * seq_tiles
    n_chunks = PROMPT_TILE // CHUNK
    levels = PROMPT_LEVELS
    mix_tile = (j - 1) % seq_tiles

    @pl.when(mix_tile == 0)
    def _():
        sret[...] = jnp.zeros_like(sret)
        shg[...] = jnp.zeros_like(shg)
        kprev[...] = jnp.zeros_like(kprev)
        vprev[...] = jnp.zeros_like(vprev)

    n_parts = 5
    part_cols = IN_COLS // n_parts

    def project(rows, part):
        cols = slice(part * part_cols, (part + 1) * part_cols)
        h_new[rows, cols] = jnp.dot(x_ref[rows, :].astype(BF16), win_ref[:, cols],
                                    preferred_element_type=F32)

    def finish(rows):
        mix = jnp.dot(cat_old[rows, :], wout_ref[...], preferred_element_type=F32)
        x1_ref[rows, :] = _layernorm(ALPHA * xold_ref[rows, :] + mix, g1_ref[...], b1_ref[...])

    def mixers(c, rows, filler):
        lb = _hg_lower_bound(lbraw_ref, layer)
        cr, sr = cr_ref[rows, :], sr_ref[rows, :]
        cs, ss = cs_ref[rows, :], ss_ref[rows, :]

        q = _rope(h_scr[rows, C_RQ:C_RQ + 128], cr, sr, RET_QK // 2, RET_QK)
        k = _rope(h_scr[rows, C_RK:C_RK + 128], cr, sr, RET_QK // 2, RET_QK) * (RET_QK ** -0.5)
        v = h_scr[rows, C_RV:C_RV + 256]
        lane_head = _iota((CHUNK, 128), 1) // RET_QK
        for hd in range(RET_HEADS):
            a = _mm_nt(jnp.where(lane_head == hd, q, 0.0), k) * dec_ref[hd]
            ar_scr[:, CHUNK * hd:CHUNK * (hd + 1)] = a.astype(BF16)
        o_r = jnp.dot(ar_scr[...], _head_blockdiag(v, CHUNK), preferred_element_type=F32)
        s0 = sret[...]
        o_r = o_r + _mm(q, s0) * inter_ref[...]
        r_i = _iota((128, 256), 0) // RET_QK
        c_i = _iota((128, 256), 1) // RET_V
        upd = _mm((k * kdec_ref[...]).T, v)
        sret[...] = rowdec_ref[...] * s0 + jnp.where(r_i == c_i, upd, 0.0)
        cat_scr[rows, 0:256] = _rms_gate(o_r, h_scr[rows, C_RG:C_RG + 256], hblk_ref).astype(BF16)
        filler(0)

        q = h_scr[rows, C_HQ:C_HQ + 256]
        lf = _hg_log_forget(h_scr[rows, C_HF:C_HF + 256], lb)
        k = 1.0 - jnp.exp(lf)
        l3 = jnp.concatenate(_split3(lf), axis=1)
        b = jnp.dot(tri_ref[...], l3, preferred_element_type=F32)
        b = b[:, 0:256] + b[:, 256:512] + b[:, 512:768]
        for l, e in enumerate(_level_exponents(b, CHUNK, levels)):
            w = jnp.exp(e)
            qs_scr[l] = (q * w).astype(BF16)
            ks_scr[l] = (k * w).astype(BF16)
        qs_scr[levels] = q.astype(BF16)
        ks_scr[levels] = k.astype(BF16)
        bl = b[CHUNK - 1:CHUNK, :]
        qb_scr[...] = (q * jnp.exp(b)).astype(BF16)
        kb_scr[...] = k * jnp.exp(bl - b)
        bl_scr[...] = jnp.broadcast_to(jnp.exp(bl), (8, 256))
        v = h_scr[rows, C_HI:C_HI + 256]
        filler(1)
        lvl = lvl_ref[...]
        for hd in range(HG_HEADS):
            lanes = slice(HG_K * hd, HG_K * (hd + 1))
            a = jnp.zeros((CHUNK, CHUNK), F32)
            for l in range(levels + 1):
                p = lax.dot_general(qs_scr[l, :, lanes], ks_scr[l, :, lanes],
                                    (((1,), (1,)), ((), ())), preferred_element_type=F32)
                a = jnp.where(lvl == l, p, a)
            ah_scr[:, CHUNK * hd:CHUNK * (hd + 1)] = a.astype(BF16)
        o_h = jnp.dot(ah_scr[...], _head_blockdiag(v, CHUNK), preferred_element_type=F32)
        s0 = shg[...]
        o_h = o_h + jnp.dot(qb_scr[...], s0.astype(BF16), preferred_element_type=F32)
        decay_t = jnp.broadcast_to(bl_scr[0:1, :], (128, 256)).T
        r_i = _iota((256, 256), 0) // HG_K
        c_i = _iota((256, 256), 1) // HG_V
        upd = _mm(kb_scr[...].T, v)
        shg[...] = jnp.tile(decay_t, (1, 2)) * s0 + jnp.where(r_i == c_i, upd, 0.0)
        cat_scr[rows, 256:512] = _rms_gate(o_h, h_scr[rows, C_HG:C_HG + 256], hblk_ref,
                                           hgw_ref[...]).astype(BF16)
        filler(2)

        k = _rope(h_scr[rows, C_SK:C_SK + 128], cs, ss, ROT_DIM // 2, SWA_HD)
        v = h_scr[rows, C_SV:C_SV + 128]
        kk = jnp.concatenate([kprev[...], k.astype(BF16)], axis=0)
        vv = jnp.concatenate([vprev[...], v.astype(BF16)], axis=0)
        mask = band_ref[...] > jnp.where(mix_tile * n_chunks + c > 0, 0, 1)
        qm = _swa_heads(h_scr[rows, C_SQ:C_SQ + 512],
                        lambda t: _rope(t, cs, ss, ROT_DIM // 2, SWA_HD))
        outs = []
        for hd in range(SWA_HEADS):
            g = hd // SWA_GROUP
            sink = sinks_ref[layer, hd]
            s = jnp.where(mask, _mm_nt(qm[hd], kk), NEG)
            m = jnp.maximum(jnp.max(s, axis=1, keepdims=True), sink)
            p = jnp.exp(s - m)
            den = jnp.sum(p, axis=1, keepdims=True) + jnp.exp(sink - m)
            outs.append(_mm(p, vv)[:, SWA_HD * g:SWA_HD * (g + 1)] / den)
            if hd % 2 == 1:
                col = 512 + LANES * (hd // 2)
                cat_scr[rows, col:col + LANES] = jnp.concatenate(outs, axis=1).astype(BF16)
                outs = []
            if hd == SWA_HEADS // 2 - 1:
                filler(3)
        filler(4)
        kprev[...] = k.astype(BF16)
        vprev[...] = v.astype(BF16)
        pk_ref[...] = k.T
        pv_ref[...] = v.T


    def run(do_project, do_mixers, do_finish):
        def body(c, carry):
            rows = pl.ds(pl.multiple_of(c * CHUNK, CHUNK), CHUNK)
            fill = (lambda part: project(rows, part)) if do_project else (lambda part: None)
            if do_mixers:
                mixers(c, rows, fill)
            elif do_project:
                for part in range(n_parts):
                    project(rows, part)
            if do_finish:
                finish(rows)
            if do_mixers:
                cat_old[rows, :] = cat_scr[rows, :]
            if do_project:
                h_scr[rows, :] = h_new[rows, :]
            return carry
        lax.fori_loop(0, n_chunks, body, 0)
        if do_mixers:
            str_ref[...] = sret[...]
            sth_ref[...] = shg[...]

    pl.when(j == 0)(lambda: run(True, False, False))
    pl.when(j == 1)(lambda: run(True, True, False))
    pl.when((j >= 2) & (j < n_tiles))(lambda: run(True, True, True))
    pl.when(j == n_tiles)(lambda: run(False, True, True))
    pl.when(j == n_tiles + 1)(lambda: run(False, False, True))


def _const_spec(shape, ngrid):
    zeros = (0,) * len(shape)
    if ngrid == 1:
        return pl.BlockSpec(shape, lambda i: zeros)
    return pl.BlockSpec(shape, lambda b, j: zeros)


def _prompt_mix(layer, x, w_in, w_out, sinks, lbraw, hgw, g1, b1, tabs, consts):
    t = PROMPT_TILE
    seq_tiles = SEQ // t
    wspec = lambda r, c: pl.BlockSpec((None, r, c), lambda j: (layer, 0, 0),
                                      pipeline_mode=pl.Buffered(1))
    vspec = lambda c: pl.BlockSpec((None, 1, c), lambda j: (layer, 0, 0))
    last = BATCH * seq_tiles - 1
    tile_new = lambda j: jnp.minimum(j, last)
    tile_mix = lambda j: jnp.clip(j - 1, 0, last)
    tile_old = lambda j: jnp.clip(j - 2, 0, last)
    tspec = pl.BlockSpec((t, LANES), lambda j: (tile_mix(j) % seq_tiles, 0))
    sspec = lambda r, c: pl.BlockSpec((None, r, c), lambda j: (tile_mix(j) // seq_tiles, 0, 0))
    in_specs = [pl.BlockSpec(memory_space=pltpu.SMEM),
                pl.BlockSpec((t, D_MODEL), lambda j: (tile_new(j), 0)),
                pl.BlockSpec((t, D_MODEL), lambda j: (tile_old(j), 0)),
                wspec(D_MODEL, IN_COLS), wspec(D_MODEL, D_MODEL),
                _const_spec((DEPTH, 256), 1), vspec(256), vspec(D_MODEL), vspec(D_MODEL),
                tspec, tspec, tspec, tspec] + [_const_spec(c.shape, 1) for c in consts]
    out_shape = [jax.ShapeDtypeStruct((BATCH * SEQ, D_MODEL), F32),
                 jax.ShapeDtypeStruct((BATCH, 128, 256), F32),
                 jax.ShapeDtypeStruct((BATCH, 256, 256), F32),
                 jax.ShapeDtypeStruct((BATCH, WINDOW, 128), F32),
                 jax.ShapeDtypeStruct((BATCH, WINDOW, 128), F32)]
    out_specs = [pl.BlockSpec((t, D_MODEL), lambda j: (tile_old(j), 0)),
                 sspec(128, 256), sspec(256, 256), sspec(WINDOW, 128), sspec(WINDOW, 128)]
    scratch = [pltpu.VMEM((t, IN_COLS), F32), pltpu.VMEM((t, IN_COLS), F32),
               pltpu.VMEM((t, D_MODEL), BF16), pltpu.VMEM((t, D_MODEL), BF16),
               pltpu.VMEM((128, 256), F32), pltpu.VMEM((256, 256), F32),
               pltpu.VMEM((WINDOW, 128), BF16), pltpu.VMEM((WINDOW, 128), BF16),
               pltpu.VMEM((PROMPT_LEVELS + 1, CHUNK, 256), BF16),
               pltpu.VMEM((PROMPT_LEVELS + 1, CHUNK, 256), BF16),
               pltpu.VMEM((CHUNK, 4 * CHUNK), BF16), pltpu.VMEM((CHUNK, 4 * CHUNK), BF16),
               pltpu.VMEM((CHUNK, 256), BF16), pltpu.VMEM((CHUNK, 256), F32),
               pltpu.VMEM((8, 256), F32)]
    return pl.pallas_call(
        functools.partial(_prompt_mix_kernel, layer=layer),
        grid=(BATCH * seq_tiles + 2,), in_specs=in_specs, out_specs=out_specs,
        out_shape=out_shape, scratch_shapes=scratch, name=f"prompt_mix_{layer}",
        compiler_params=pltpu.CompilerParams(dimension_semantics=("arbitrary",),
                                             vmem_limit_bytes=VMEM_LIMIT),
    )(sinks, x, x, w_in, w_out, lbraw, hgw, g1, b1, *tabs, *consts)


def _ffn_kernel(x_ref, wup_ref, wdn_ref, g_ref, b_ref, o_ref):
    x = x_ref[...]
    xb = x.astype(BF16)
    acc = jnp.zeros(x.shape, F32)
    step = 1024
    for i in range(D_FF // step):
        hid = jnp.maximum(jnp.dot(xb, wup_ref[:, i * step:(i + 1) * step],
                                  preferred_element_type=F32), 0.0)
        acc = acc + jnp.dot((hid * hid).astype(BF16), wdn_ref[i * step:(i + 1) * step, :],
                            preferred_element_type=F32)
    o_ref[...] = _layernorm(ALPHA * x + acc, g_ref[...], b_ref[...])


def _ffn(layer, x, w_up, w_down, g2, b2):
    rows = x.shape[0]
    t = FFN_TILE
    return pl.pallas_call(
        _ffn_kernel,
        grid=(rows // t,),
        in_specs=[pl.BlockSpec((t, D_MODEL), lambda i: (i, 0)),
                  pl.BlockSpec((None, D_MODEL, D_FF), lambda i: (layer, 0, 0),
                               pipeline_mode=pl.Buffered(1)),
                  pl.BlockSpec((None, D_FF, D_MODEL), lambda i: (layer, 0, 0),
                               pipeline_mode=pl.Buffered(1)),
                  pl.BlockSpec((None, 1, D_MODEL), lambda i: (layer, 0, 0)),
                  pl.BlockSpec((None, 1, D_MODEL), lambda i: (layer, 0, 0))],
        out_specs=pl.BlockSpec((t, D_MODEL), lambda i: (i, 0)),
        out_shape=jax.ShapeDtypeStruct((rows, D_MODEL), F32),
        name=f"ffn_{layer}_{rows}",
        compiler_params=pltpu.CompilerParams(dimension_semantics=("arbitrary",),
                                             vmem_limit_bytes=VMEM_LIMIT),
    )(x, w_up, w_down, g2, b2)


def _sample_mix_kernel(sinks_ref, x_ref, win_ref, wout_ref, lbraw_ref, hgw_ref, g1_ref, b1_ref,
                       cr_ref, sr_ref, cs_ref, ss_ref, sret_ref, shg_ref, ck_ref, cv_ref,
                       dec_ref, inter_ref, kdec_ref, mall_ref, hblk_ref,
                       nsret_all, nshg_all, nk_all, nv_all,
                       x1_ref, nsret_ref, nshg_ref, nk_ref, nv_ref, os_scr, *, layer, gam4):
    del nsret_all, nshg_all, nk_all, nv_all
    rows, seqs = SAMPLE_ROWS, SAMPLE_SEQS
    x = x_ref[...]
    h = jnp.dot(x.astype(BF16), win_ref[...], preferred_element_type=F32)
    cr, sr, cs, ss = cr_ref[...], sr_ref[...], cs_ref[...], ss_ref[...]
    lb = _hg_lower_bound(lbraw_ref, layer)

    q = _rope(h[:, C_RQ:C_RQ + 128], cr, sr, RET_QK // 2, RET_QK)
    k = _rope(h[:, C_RK:C_RK + 128], cr, sr, RET_QK // 2, RET_QK) * (RET_QK ** -0.5)
    v = h[:, C_RV:C_RV + 256]
    o_r = _ret_intra(q, k, v, dec_ref, rows)
    hk = RET_HEADS * RET_QK
    s0 = sret_ref[...].reshape(seqs * hk, RET_V)
    r_i = (_iota((seqs * hk, 256), 0) % hk) // RET_QK
    c_i = _iota((seqs * hk, 256), 1) // RET_V
    s0bd = jnp.where(r_i == c_i, jnp.tile(s0, (1, 4)), 0.0).astype(BF16)
    own = _iota((rows, seqs * hk), 0) // DEC_SEQ == _iota((rows, seqs * hk), 1) // hk
    o_r = o_r + _mm(jnp.where(own, jnp.tile(q, (1, seqs)), 0.0), s0bd) * inter_ref[...]
    kdt = (k * kdec_ref[...]).T
    col_seq = _iota((hk, rows), 1) // DEC_SEQ
    for n in range(seqs):
        upd = _mm(jnp.where(col_seq == n, kdt, 0.0), v)
        for hd in range(RET_HEADS):
            nsret_ref[n, hd] = (gam4[hd] * sret_ref[n, hd]
                                + upd[RET_QK * hd:RET_QK * (hd + 1), RET_V * hd:RET_V * (hd + 1)])
    ret = _rms_gate(o_r, h[:, C_RG:C_RG + 256], hblk_ref)

    q = h[:, C_HQ:C_HQ + 256]
    lf = _hg_log_forget(h[:, C_HF:C_HF + 256], lb)
    k = 1.0 - jnp.exp(lf)
    v = h[:, C_HI:C_HI + 256]
    o_h, b, tail = _hgrn_intra(q, k, v, lf, mall_ref, hblk_ref, rows, SAMPLE_LEVELS)
    hk = HG_HEADS * HG_K
    s0 = shg_ref[...].reshape(seqs * hk, HG_V)
    r_i = (_iota((seqs * hk, 256), 0) % hk) // HG_K
    c_i = _iota((seqs * hk, 256), 1) // HG_V
    s0bd = jnp.where(r_i == c_i, jnp.tile(s0, (1, 4)), 0.0).astype(BF16)
    own = _iota((rows, seqs * hk), 0) // DEC_SEQ == _iota((rows, seqs * hk), 1) // hk
    o_h = o_h + _mm(jnp.where(own, jnp.tile(q * jnp.exp(b), (1, seqs)), 0.0), s0bd)
    kdt = (k * jnp.exp(tail)).T
    ebt = jnp.exp(b).T
    col_seq = _iota((hk, rows), 1) // DEC_SEQ
    for n in range(seqs):
        upd = _mm(jnp.where(col_seq == n, kdt, 0.0), v)
        last = DEC_SEQ * n + DEC_SEQ - 1
        dcol = ebt[:, last:last + 1]
        for hd in range(HG_HEADS):
            nshg_ref[n, hd] = (dcol[HG_K * hd:HG_K * (hd + 1)] * shg_ref[n, hd]
                               + upd[HG_K * hd:HG_K * (hd + 1), HG_V * hd:HG_V * (hd + 1)])
    hg = _rms_gate(o_h, h[:, C_HG:C_HG + 256], hblk_ref, hgw_ref[...])

    k = _rope(h[:, C_SK:C_SK + 128], cs, ss, ROT_DIM // 2, SWA_HD)
    v = h[:, C_SV:C_SV + 128]
    kb, vb = k.astype(BF16), v.astype(BF16)
    qm = _swa_heads(h[:, C_SQ:C_SQ + 512], lambda t: _rope(t, cs, ss, ROT_DIM // 2, SWA_HD))
    r64 = _iota((64, 1), 0)
    sinkcol = jnp.zeros((64, 1), F32)
    for hd in range(SWA_HEADS):
        sinkcol = jnp.where(r64 // 8 == hd, sinks_ref[layer, hd], sinkcol)
    rc = _iota((64, 128), 0)
    cc = _iota((64, 128), 1)
    first_seq = (rc % 8) < DEC_SEQ
    cmask = cc >= rc % DEC_SEQ
    rn = _iota((64, rows), 0)
    cn = _iota((64, rows), 1)
    new_lanes = _iota((128, 128), 1) >= WINDOW - DEC_SEQ
    pad_rows = jnp.zeros((WINDOW - 8, 128), F32)
    for pr in range(seqs // 2):
        na, nb = 2 * pr, 2 * pr + 1
        qp = jnp.concatenate([t[8 * pr:8 * pr + 8] for t in qm], axis=0)
        ka, kb_c = ck_ref[na], ck_ref[nb]
        va, vb_c = cv_ref[na], cv_ref[nb]
        sc = jnp.where(first_seq, _mm(qp, ka), _mm(qp, kb_c))
        sc = jnp.where(cmask, sc, NEG)
        nmask = (cn // DEC_SEQ == 2 * pr + (rn % 8) // DEC_SEQ) & (cn % DEC_SEQ <= rn % DEC_SEQ)
        sn = jnp.where(nmask, _mm_nt(qp, kb), NEG)
        m = jnp.maximum(jnp.maximum(jnp.max(sc, axis=1, keepdims=True),
                                    jnp.max(sn, axis=1, keepdims=True)), sinkcol)
        pc = jnp.exp(sc - m)
        pn = jnp.exp(sn - m)
        den = (jnp.sum(pc, axis=1, keepdims=True) + jnp.sum(pn, axis=1, keepdims=True)
               + jnp.exp(sinkcol - m))
        o = (_mm_nt(jnp.where(first_seq, pc, 0.0), va) + _mm_nt(jnp.where(first_seq, 0.0, pc), vb_c)
             + _mm(pn, vb)) / den
        pieces = []
        for hd in range(SWA_HEADS):
            g = hd // SWA_GROUP
            pieces.append(o[8 * hd:8 * hd + 8, SWA_HD * g:SWA_HD * (g + 1)])
        os_scr[8 * pr:8 * pr + 8, :] = jnp.concatenate(pieces, axis=1)
        for new, old_a, old_b, dst in ((k[8 * pr:8 * pr + 8], ka, kb_c, nk_ref),
                                       (v[8 * pr:8 * pr + 8], va, vb_c, nv_ref)):
            new_t = jnp.concatenate([new, pad_rows], axis=0).T
            dst[na] = jnp.where(new_lanes, pltpu.roll(new_t, WINDOW - DEC_SEQ, 1),
                                pltpu.roll(old_a, WINDOW - DEC_SEQ, 1))
            dst[nb] = jnp.where(new_lanes, pltpu.roll(new_t, WINDOW - 2 * DEC_SEQ, 1),
                                pltpu.roll(old_b, WINDOW - DEC_SEQ, 1))

    cat = jnp.concatenate([ret, hg, os_scr[...]], axis=1).astype(BF16)
    mix = jnp.dot(cat, wout_ref[...], preferred_element_type=F32)
    x1_ref[...] = _layernorm(ALPHA * x + mix, g1_ref[...], b1_ref[...])


def _sample_mix(layer, x, w_in, w_out, sinks, lbraw, hgw, g1, b1, tabs, state_ret, state_hgrn,
                cache_k, cache_v, consts, gam4, stacked):
    r, s = SAMPLE_ROWS, SAMPLE_SEQS
    wspec = lambda a, c: pl.BlockSpec((None, a, c), lambda i: (layer, 0, 0),
                                      pipeline_mode=pl.Buffered(1))
    vspec = lambda c: pl.BlockSpec((None, 1, c), lambda i: (layer, 0, 0))
    in_specs = [pl.BlockSpec(memory_space=pltpu.SMEM),
                pl.BlockSpec((r, D_MODEL), lambda i: (i, 0)),
                wspec(D_MODEL, IN_COLS), wspec(D_MODEL, D_MODEL),
                _const_spec((DEPTH, 256), 1), vspec(256), vspec(D_MODEL), vspec(D_MODEL)]
    in_specs += [_const_spec((r, LANES), 1)] * 4
    in_specs += [pl.BlockSpec((None, s, RET_HEADS, RET_QK, RET_V), lambda i: (layer, i, 0, 0, 0)),
                 pl.BlockSpec((None, s, HG_HEADS, HG_K, HG_V), lambda i: (layer, i, 0, 0, 0)),
                 pl.BlockSpec((None, s, WINDOW, 128), lambda i: (layer, i, 0, 0)),
                 pl.BlockSpec((None, s, WINDOW, 128), lambda i: (layer, i, 0, 0))]
    in_specs += [_const_spec(c.shape, 1) for c in consts]
    first_stacked = len(in_specs)
    in_specs += [pl.BlockSpec(memory_space=pl.ANY)] * len(stacked)
    out_shape = [jax.ShapeDtypeStruct((DEC_BATCH * DEC_SEQ, D_MODEL), F32)]
    out_shape += [jax.ShapeDtypeStruct(a.shape, a.dtype) for a in stacked]
    out_specs = [pl.BlockSpec((r, D_MODEL), lambda i: (i, 0)),
                 pl.BlockSpec((None, s, RET_HEADS, RET_QK, RET_V), lambda i: (layer, i, 0, 0, 0)),
                 pl.BlockSpec((None, s, HG_HEADS, HG_K, HG_V), lambda i: (layer, i, 0, 0, 0)),
                 pl.BlockSpec((None, s, WINDOW, 128), lambda i: (layer, i, 0, 0)),
                 pl.BlockSpec((None, s, WINDOW, 128), lambda i: (layer, i, 0, 0))]
    return pl.pallas_call(
        functools.partial(_sample_mix_kernel, layer=layer, gam4=gam4),
        grid=(DEC_BATCH // s,), in_specs=in_specs, out_specs=out_specs, out_shape=out_shape,
        input_output_aliases={first_stacked + i: 1 + i for i in range(len(stacked))},
        scratch_shapes=[pltpu.VMEM((r, 512), F32)], name=f"sample_mix_{layer}",
        compiler_params=pltpu.CompilerParams(dimension_semantics=("arbitrary",),
                                             vmem_limit_bytes=VMEM_LIMIT),
    )(sinks, x, w_in, w_out, lbraw, hgw, g1, b1, *tabs, state_ret, state_hgrn, cache_k, cache_v,
      *consts, *stacked)


def _decay_levels(rows, seqlen, levels):
    t = np.arange(rows)[:, None]
    u = np.arange(rows)[None, :]
    same = (t // seqlen) == (u // seqlen)
    blocks = [same & (u <= t)]
    for l in range(levels):
        m = 1 << l
        r = (t // (2 * m)) * (2 * m) + m
        lower = t >= r
        blocks.append(np.where(lower, (u >= r) & (u <= t), (u > t) & (u <= r - 1)))
    blocks.append(same & (u > t))
    return jnp.asarray(np.concatenate(blocks, axis=0).astype(np.float32), dtype=BF16)


def _rope_tables(pos):
    pos = pos.astype(F32)[:, None]
    inv = RET_THETA ** (-jnp.linspace(0.0, 1.0, RET_QK // 2, dtype=F32))
    ang = pos * inv[None, :]
    cr = jnp.tile(jnp.concatenate([jnp.cos(ang), jnp.cos(ang)], -1), (1, RET_HEADS))
    sr = jnp.tile(jnp.concatenate([-jnp.sin(ang), jnp.sin(ang)], -1), (1, RET_HEADS))
    inv = ROPE_THETA ** (-jnp.arange(0, ROT_DIM, 2, dtype=F32) / ROT_DIM)
    ang = pos * inv[None, :]
    rest = SWA_HD - ROT_DIM
    ones = jnp.ones((ang.shape[0], rest), F32)
    zeros = jnp.zeros((ang.shape[0], rest), F32)
    cs = jnp.tile(jnp.concatenate([jnp.cos(ang), jnp.cos(ang), ones], -1), (1, 2))
    ss = jnp.tile(jnp.concatenate([-jnp.sin(ang), jnp.sin(ang), zeros], -1), (1, 2))
    return cr, sr, cs, ss


def _ret_tables(tok, same):
    lg = jnp.log(1.0 - 2.0 ** (-5.0 - jnp.arange(RET_HEADS, dtype=F32)))
    n_tok = int(tok.max()) + 1
    tokf = jnp.asarray(tok, F32)
    diff = tokf[:, None] - tokf[None, :]
    dec = jnp.where(jnp.asarray(same) & (diff >= 0),
                    jnp.exp(jnp.maximum(diff, 0.0)[None] * lg[:, None, None]), 0.0)
    inter = jnp.repeat(jnp.exp((tokf + 1.0)[:, None] * lg[None, :]), RET_V, axis=1)
    kdec = jnp.repeat(jnp.exp((n_tok - 1.0 - tokf)[:, None] * lg[None, :]), RET_QK, axis=1)
    total = jnp.exp(n_tok * lg)
    return dec, inter, kdec, total


def kernel(x_prompt, x_sample, state_ret, state_hgrn, cache_swa_k, cache_swa_v, w_in, w_out,
           swa_sinks, hg_lb_raw, hg_norm_w, ln_mix_g, ln_mix_b, w_up, w_down, ln_ffn_g, ln_ffn_b):
    w_in_b, w_out_b = w_in.astype(BF16), w_out.astype(BF16)
    w_up_b, w_down_b = w_up.astype(BF16), w_down.astype(BF16)
    hgw = jnp.tile(hg_norm_w, (1, HG_HEADS)).reshape(DEPTH, 1, HG_HEADS * HG_V)
    g1, b1 = ln_mix_g.reshape(DEPTH, 1, D_MODEL), ln_mix_b.reshape(DEPTH, 1, D_MODEL)
    g2, b2 = ln_ffn_g.reshape(DEPTH, 1, D_MODEL), ln_ffn_b.reshape(DEPTH, 1, D_MODEL)
    cache_k = cache_swa_k.transpose(0, 1, 3, 4, 2).reshape(DEPTH, DEC_BATCH, 128, WINDOW)
    cache_v = cache_swa_v.transpose(0, 1, 3, 4, 2).reshape(DEPTH, DEC_BATCH, 128, WINDOW)

    hblk = jnp.asarray((np.arange(256)[:, None] // 64 == np.arange(256)[None, :] // 64)
                       .astype(np.float32), dtype=BF16)
    p_tabs = _rope_tables(jnp.arange(SEQ))
    tok = np.arange(CHUNK)
    dec, inter, kdec, total = _ret_tables(tok, np.ones((CHUNK, CHUNK), bool))
    rowdec = jnp.broadcast_to(jnp.repeat(total, RET_QK)[:, None], (128, 256))
    t_i, s_i = np.arange(CHUNK)[:, None], np.arange(CHUNK)[None, :]
    tri = jnp.asarray((s_i <= t_i).astype(np.float32), dtype=BF16)
    lvl = np.where(s_i < t_i, np.floor(np.log2(np.maximum(t_i ^ s_i, 1))),
                   np.where(s_i == t_i, PROMPT_LEVELS, -1)).astype(np.int32)
    q_i, k_j = np.arange(CHUNK)[:, None], np.arange(2 * CHUNK)[None, :]
    band = np.where((k_j >= q_i) & (k_j <= q_i + WINDOW), np.where(k_j >= WINDOW, 2, 1), 0)
    p_consts = (dec, inter, kdec, rowdec, tri, hblk, jnp.asarray(lvl),
                jnp.asarray(band.astype(np.int32)))
    row = np.arange(SAMPLE_ROWS)
    s_tabs = _rope_tables(PAST_LEN + jnp.asarray(row % DEC_SEQ))
    same = (row[:, None] // DEC_SEQ) == (row[None, :] // DEC_SEQ)
    dec_s, inter_s, kdec_s, total_s = _ret_tables(row % DEC_SEQ, same)
    s_consts = (dec_s, inter_s, kdec_s, _decay_levels(SAMPLE_ROWS, DEC_SEQ, SAMPLE_LEVELS), hblk)
    gam4 = tuple(float((1.0 - 2.0 ** (-5.0 - h)) ** DEC_SEQ) for h in range(RET_HEADS))
    del total_s

    xp = x_prompt.reshape(BATCH * SEQ, D_MODEL)
    xs = x_sample.reshape(DEC_BATCH * DEC_SEQ, D_MODEL)
    outs = [[] for _ in range(4)]
    stacked = (jnp.zeros((DEPTH, DEC_BATCH, RET_HEADS, RET_QK, RET_V), F32),
               jnp.zeros((DEPTH, DEC_BATCH, HG_HEADS, HG_K, HG_V), F32),
               jnp.zeros((DEPTH, DEC_BATCH, 128, WINDOW), F32),
               jnp.zeros((DEPTH, DEC_BATCH, 128, WINDOW), F32))
    for l in range(DEPTH):
        x1, st_r, st_h, pk, pv = _prompt_mix(l, xp, w_in_b, w_out_b, swa_sinks, hg_lb_raw, hgw,
                                             g1, b1, p_tabs, p_consts)
        xp = _ffn(l, x1, w_up_b, w_down_b, g2, b2)
        outs[0].append(jnp.stack([st_r[:, RET_QK * h:RET_QK * (h + 1), RET_V * h:RET_V * (h + 1)]
                                  for h in range(RET_HEADS)], axis=1))
        outs[1].append(jnp.stack([st_h[:, HG_K * h:HG_K * (h + 1), HG_V * h:HG_V * (h + 1)]
                                  for h in range(HG_HEADS)], axis=1))
        outs[2].append(pk.reshape(BATCH, SWA_KV, SWA_HD, WINDOW).transpose(0, 3, 1, 2))
        outs[3].append(pv.reshape(BATCH, SWA_KV, SWA_HD, WINDOW).transpose(0, 3, 1, 2))

        x1, *stacked = _sample_mix(l, xs, w_in_b, w_out_b, swa_sinks, hg_lb_raw, hgw, g1, b1,
                                   s_tabs, state_ret, state_hgrn, cache_k, cache_v, s_consts,
                                   gam4, tuple(stacked))
        xs = _ffn(l, x1, w_up_b, w_down_b, g2, b2)

    ns_r, ns_h, nk, nv = stacked
    nk = nk.reshape(DEPTH, DEC_BATCH, SWA_KV, SWA_HD, WINDOW).transpose(0, 1, 4, 2, 3)
    nv = nv.reshape(DEPTH, DEC_BATCH, SWA_KV, SWA_HD, WINDOW).transpose(0, 1, 4, 2, 3)
    return ((xp.reshape(BATCH, SEQ, D_MODEL), xs.reshape(DEC_BATCH, DEC_SEQ, D_MODEL))
            + tuple(jnp.stack(o) for o in outs) + (ns_r, ns_h, nk, nv))
```

```python
import functools

import numpy as np
import jax
import jax.numpy as jnp
from jax import lax
from jax.experimental import pallas as pl
from jax.experimental.pallas import tpu as pltpu

F32 = jnp.float32
BF16 = jnp.bfloat16

D_MODEL = 1024
BATCH = 2
SEQ = 8192
DEPTH = 4
DEC_BATCH = 128
DEC_SEQ = 4
PAST_LEN = 8192
RET_HEADS = 4
RET_QK = 32
RET_V = 64
RET_THETA = 10000.0
HG_HEADS = 4
HG_K = 64
HG_V = 64
SWA_HEADS = 8
SWA_KV = 2
SWA_GROUP = SWA_HEADS // SWA_KV
SWA_HD = 64
WINDOW = 128
ROT_DIM = 16
ROPE_THETA = 500000.0
CHUNK = 128
D_FF = 4 * D_MODEL
IN_COLS = 2560
ALPHA = (2.0 * DEPTH) ** 0.25
LN_EPS = 1e-5
RMS_EPS = 1e-6
NEG = -1e30

C_RQ, C_RK, C_RV, C_RG = 0, 128, 256, 512
C_HQ, C_HF, C_HI, C_HG = 768, 1024, 1280, 1536
C_SQ, C_SK, C_SV = 1792, 2304, 2432

LANES = 128
PROMPT_TILE = 512
FFN_TILE = 512
SAMPLE_SEQS = 16
SAMPLE_ROWS = SAMPLE_SEQS * DEC_SEQ
PROMPT_LEVELS = 7
SAMPLE_LEVELS = 2
VMEM_LIMIT = 56 * 1024 * 1024


def _mm(a, b):
    return jnp.dot(a.astype(BF16), b.astype(BF16), preferred_element_type=F32)


def _mm_nt(a, b):
    return lax.dot_general(a.astype(BF16), b.astype(BF16), (((1,), (1,)), ((), ())),
                           preferred_element_type=F32)


def _iota(shape, dim):
    return lax.broadcasted_iota(jnp.int32, shape, dim)


def _layernorm(x, g, b):
    mu = jnp.mean(x, axis=-1, keepdims=True)
    xc = x - mu
    var = jnp.mean(xc * xc, axis=-1, keepdims=True)
    return xc * lax.rsqrt(var + LN_EPS) * g + b


def _rope(x, cos_t, sin_t, half, period):
    first_half = (_iota((1, LANES), 1) % period) < half
    fwd = pltpu.roll(x, LANES - half, 1)
    bwd = pltpu.roll(x, half, 1)
    return x * cos_t + jnp.where(first_half, fwd, bwd) * sin_t


def _silu(x):
    return x * (1.0 / (1.0 + jnp.exp(-x)))


def _split3(x):
    hi = x.astype(BF16)
    r = x - hi.astype(F32)
    mid = r.astype(BF16)
    lo = (r - mid.astype(F32)).astype(BF16)
    return hi, mid, lo


def _hg_lower_bound(lbraw_ref, layer):
    raw = lbraw_ref[...]
    e = jnp.exp(raw - jnp.max(raw, axis=0, keepdims=True))
    sm = e / jnp.sum(e, axis=0, keepdims=True)
    cum = sm[0:1]
    for i in range(1, layer + 1):
        cum = cum + sm[i:i + 1]
    return cum - sm[0:1]


def _hg_log_forget(hf, lb):
    logsig = jnp.minimum(hf, 0.0) - jnp.log(1.0 + jnp.exp(-jnp.abs(hf)))
    a = jnp.log(lb)
    b = jnp.log(1.0 - lb) + logsig
    return jnp.maximum(a, b) + jnp.log(1.0 + jnp.exp(-jnp.abs(a - b)))


def _level_exponents(b, rows, levels):
    g = rows // 8
    n = b.shape[1]
    b3 = b.reshape(g, 8, n)
    sub = _iota((g, 8, n), 1)
    dn1 = pltpu.roll(b3, 1, 1)
    dn2 = pltpu.roll(b3, 2, 1)
    up1 = pltpu.roll(b3, 7, 1)
    out = [jnp.where(sub % 2 == 1, b3 - dn1, 0.0)]
    m4 = sub % 4
    out.append(jnp.where(m4 == 0, up1 - b3,
                         jnp.where(m4 == 1, 0.0, jnp.where(m4 == 2, b3 - dn1, b3 - dn2))))
    if levels > 2:
        r3 = jnp.broadcast_to(b3[:, 3:4, :], b3.shape)
        out.append(jnp.where(sub < 4, r3 - b3, b3 - r3))
        r7 = jnp.broadcast_to(b3[:, 7:8, :], b3.shape)
        grp = _iota((g, 8, n), 0)
        for l in range(3, levels):
            span = (2 << l) // 8
            half = span // 2
            ref = r7.reshape(g // span, span, 8, n)[:, half - 1:half]
            ref = jnp.broadcast_to(ref, (g // span, span, 8, n)).reshape(g, 8, n)
            out.append(jnp.where(grp % span >= half, b3 - ref, ref - b3))
    return [e.reshape(rows, n) for e in out]


def _head_blockdiag(v, rows):
    r = _iota((4 * rows, 256), 0) // rows
    c = _iota((4 * rows, 256), 1) // 64
    return jnp.where(r == c, jnp.tile(v, (4, 1)), 0.0).astype(BF16)


def _intra_out(a_ref, v, rows):
    r = _iota((2 * rows, LANES), 0) // rows
    c = _iota((2 * rows, LANES), 1) // 64
    cols = []
    for pair in range(2):
        vp = jnp.tile(v[:, LANES * pair:LANES * (pair + 1)], (2, 1))
        vp = jnp.where(r == c, vp, 0.0).astype(BF16)
        cols.append(jnp.dot(a_ref[:, 2 * rows * pair:2 * rows * (pair + 1)], vp,
                            preferred_element_type=F32))
    return jnp.concatenate(cols, axis=1)


def _rms_gate(o, gate, hblk_ref, w=None):
    ms = _mm(o * o, hblk_ref[...]) * (1.0 / 64.0)
    y = o * lax.rsqrt(ms + RMS_EPS)
    if w is not None:
        y = y * w
    return y * _silu(gate)


def _ret_intra(q, k, v, dec_ref, rows):
    lane_head = _iota((rows, 128), 1) // RET_QK
    a = [_mm_nt(jnp.where(lane_head == h, q, 0.0), k) * dec_ref[h] for h in range(RET_HEADS)]
    return _mm(jnp.concatenate(a, axis=1), _head_blockdiag(v, rows))


def _hgrn_intra(q, k, v, lf, mall_ref, hblk_ref, rows, levels):
    l3 = jnp.concatenate(_split3(lf), axis=1)
    eb = jnp.dot(mall_ref[...], l3, preferred_element_type=F32)
    eb = eb[:, 0:256] + eb[:, 256:512] + eb[:, 512:768]
    b = eb[0:rows]
    tail = eb[(levels + 1) * rows:(levels + 2) * rows]
    row = _iota((rows, 256), 0)
    lane_head = _iota((rows, 256), 1) // HG_K
    ra = _iota((rows, rows), 0)
    ca = _iota((rows, rows), 1)
    acc = [jnp.zeros((rows, rows), F32) for _ in range(HG_HEADS)]
    for l in range(levels):
        w = jnp.exp(eb[(l + 1) * rows:(l + 2) * rows])
        lower = ((row >> l) & 1) == 1
        qs = jnp.where(lower, q * w, 0.0).astype(BF16)
        ks = jnp.where(lower, 0.0, k * w).astype(BF16)
        same_block = (ra >> (l + 1)) == (ca >> (l + 1))
        for h in range(HG_HEADS):
            p = _mm_nt(jnp.where(lane_head == h, qs, jnp.zeros_like(qs)), ks)
            acc[h] = acc[h] + jnp.where(same_block, p, 0.0)
    o = _mm(jnp.concatenate(acc, axis=1), _head_blockdiag(v, rows))
    o = o + _mm(q * k, hblk_ref[...]) * v
    return o, b, tail


def _swa_heads(qcols, rope_q, scale=SWA_HD ** -0.5):
    out = []
    for c in range(4):
        qc = rope_q(qcols[:, LANES * c:LANES * (c + 1)])
        if scale is not None:
            qc = qc * scale
        half = _iota(qc.shape, 1) // SWA_HD
        for p in range(2):
            g = (2 * c + p) // SWA_GROUP
            qm = jnp.where(half == p, qc, 0.0)
            if p != g:
                qm = pltpu.roll(qm, SWA_HD, 1)
            out.append(qm)
    return out


def _prompt_mix_kernel(sinks_ref, x_ref, xold_ref, win_ref, wout_ref, lbraw_ref, hgw_ref, g1_ref,
                       b1_ref, cr_ref, sr_ref, cs_ref, ss_ref, dec_ref, inter_ref, kdec_ref,
                       rowdec_ref, tri_ref, hblk_ref, lvl_ref, band_ref, wup_f32, wdn_f32,
                       x1_ref, str_ref, sth_ref, pk_ref, pv_ref, wup_b, wdn_b,
                       h_new, h_scr, cat_scr, cat_old, sret, shg, kprev, vprev, qs_scr, ks_scr,
                       ar_scr, ah_scr, qb_scr, kb_scr, bl_scr, *, layer):
    j = pl.program_id(0)
    seq_tiles = SEQ // PROMPT_TILE
    n_tiles = BATCH * seq_tiles
    n_chunks = PROMPT_TILE // CHUNK
    levels = PROMPT_LEVELS
    mix_tile = (j - 1) % seq_tiles

    wup_b[...] = wup_f32[...].astype(BF16)
    wdn_b[...] = wdn_f32[...].astype(BF16)

    @pl.when(mix_tile == 0)
    def _():
        sret[...] = jnp.zeros_like(sret)
        shg[...] = jnp.zeros_like(shg)
        kprev[...] = jnp.zeros_like(kprev)
        vprev[...] = jnp.zeros_like(vprev)

    n_parts = 5
    part_cols = IN_COLS // n_parts

    def project(rows, part):
        cols = slice(part * part_cols, (part + 1) * part_cols)
        h_new[rows, cols] = jnp.dot(x_ref[rows, :].astype(BF16), win_ref[:, cols],
                                    preferred_element_type=F32)

    def finish(rows):
        mix = jnp.dot(cat_old[rows, :], wout_ref[...], preferred_element_type=F32)
        x1_ref[rows, :] = _layernorm(ALPHA * xold_ref[rows, :] + mix, g1_ref[...], b1_ref[...])

    def mixers(c, rows, filler):
        lb = _hg_lower_bound(lbraw_ref, layer)
        cr, sr = cr_ref[rows, :], sr_ref[rows, :]
        cs, ss = cs_ref[rows, :], ss_ref[rows, :]

        q = _rope(h_scr[rows, C_RQ:C_RQ + 128], cr, sr, RET_QK // 2, RET_QK)
        k = _rope(h_scr[rows, C_RK:C_RK + 128], cr, sr, RET_QK // 2, RET_QK) * (RET_QK ** -0.5)
        v = h_scr[rows, C_RV:C_RV + 256]
        lane_head = _iota((CHUNK, 128), 1) // RET_QK
        for hd in range(RET_HEADS):
            a = _mm_nt(jnp.where(lane_head == hd, q, 0.0), k) * dec_ref[hd]
            ar_scr[:, CHUNK * hd:CHUNK * (hd + 1)] = a.astype(BF16)
        o_r = jnp.dot(ar_scr[...], _head_blockdiag(v, CHUNK), preferred_element_type=F32)
        s0 = sret[...]
        o_r = o_r + _mm(q, s0) * inter_ref[...]
        r_i = _iota((128, 256), 0) // RET_QK
        c_i = _iota((128, 256), 1) // RET_V
        upd = _mm((k * kdec_ref[...]).T, v)
        sret[...] = rowdec_ref[...] * s0 + jnp.where(r_i == c_i, upd, 0.0)
        cat_scr[rows, 0:256] = _rms_gate(o_r, h_scr[rows, C_RG:C_RG + 256], hblk_ref).astype(BF16)
        filler(0)

        q = h_scr[rows, C_HQ:C_HQ + 256]
        lf = _hg_log_forget(h_scr[rows, C_HF:C_HF + 256], lb)
        k = 1.0 - jnp.exp(lf)
        l3 = jnp.concatenate(_split3(lf), axis=1)
        b = jnp.dot(tri_ref[...], l3, preferred_element_type=F32)
        b = b[:, 0:256] + b[:, 256:512] + b[:, 512:768]
        for l, e in enumerate(_level_exponents(b, CHUNK, levels)):
            w = jnp.exp(e)
            qs_scr[l] = (q * w).astype(BF16)
            ks_scr[l] = (k * w).astype(BF16)
        qs_scr[levels] = q.astype(BF16)
        ks_scr[levels] = k.astype(BF16)
        bl = b[CHUNK - 1:CHUNK, :]
        qb_scr[...] = (q * jnp.exp(b)).astype(BF16)
        kb_scr[...] = k * jnp.exp(bl - b)
        bl_scr[...] = jnp.broadcast_to(jnp.exp(bl), (8, 256))
        v = h_scr[rows, C_HI:C_HI + 256]
        filler(1)
        lvl = lvl_ref[...]
        for hd in range(HG_HEADS):
            lanes = slice(HG_K * hd, HG_K * (hd + 1))
            a = jnp.zeros((CHUNK, CHUNK), F32)
            for l in range(levels + 1):
                p = lax.dot_general(qs_scr[l, :, lanes], ks_scr[l, :, lanes],
                                    (((1,), (1,)), ((), ())), preferred_element_type=F32)
                a = jnp.where(lvl == l, p, a)
            ah_scr[:, CHUNK * hd:CHUNK * (hd + 1)] = a.astype(BF16)
        o_h = jnp.dot(ah_scr[...], _head_blockdiag(v, CHUNK), preferred_element_type=F32)
        s0 = shg[...]
        o_h = o_h + jnp.dot(qb_scr[...], s0.astype(BF16), preferred_element_type=F32)
        decay_t = jnp.broadcast_to(bl_scr[0:1, :], (128, 256)).T
        r_i = _iota((256, 256), 0) // HG_K
        c_i = _iota((256, 256), 1) // HG_V
        upd = _mm(kb_scr[...].T, v)
        shg[...] = jnp.tile(decay_t, (1, 2)) * s0 + jnp.where(r_i == c_i, upd, 0.0)
        cat_scr[rows, 256:512] = _rms_gate(o_h, h_scr[rows, C_HG:C_HG + 256], hblk_ref,
                                           hgw_ref[...]).astype(BF16)
        filler(2)

        k = _rope(h_scr[rows, C_SK:C_SK + 128], cs, ss, ROT_DIM // 2, SWA_HD)
        v = h_scr[rows, C_SV:C_SV + 128]
        ks_b = (k * (SWA_HD ** -0.5)).astype(BF16)
        kk = jnp.concatenate([kprev[...], ks_b], axis=0)
        vv = jnp.concatenate([vprev[...], v.astype(BF16)], axis=0)
        mask = band_ref[...] > jnp.where(mix_tile * n_chunks + c > 0, 0, 1)
        qm = _swa_heads(h_scr[rows, C_SQ:C_SQ + 512],
                        lambda t: _rope(t, cs, ss, ROT_DIM // 2, SWA_HD), scale=None)
        outs = []
        for hd in range(SWA_HEADS):
            g = hd // SWA_GROUP
            sink = sinks_ref[layer, hd]
            s = jnp.where(mask, _mm_nt(qm[hd], kk), NEG)
            m = jnp.maximum(jnp.max(s, axis=1, keepdims=True), sink)
            p = jnp.exp(s - m)
            den = jnp.sum(p, axis=1, keepdims=True) + jnp.exp(sink - m)
            outs.append(_mm(p, vv)[:, SWA_HD * g:SWA_HD * (g + 1)] / den)
            if hd % 2 == 1:
                col = 512 + LANES * (hd // 2)
                cat_scr[rows, col:col + LANES] = jnp.concatenate(outs, axis=1).astype(BF16)
                outs = []
            if hd == SWA_HEADS // 2 - 1:
                filler(3)
        filler(4)
        kprev[...] = ks_b
        vprev[...] = v.astype(BF16)
        pk_ref[...] = k.T
        pv_ref[...] = v.T


    def run(do_project, do_mixers, do_finish):
        def body(c, carry):
            rows = pl.ds(pl.multiple_of(c * CHUNK, CHUNK), CHUNK)
            fill = (lambda part: project(rows, part)) if do_project else (lambda part: None)
            if do_mixers:
                mixers(c, rows, fill)
            elif do_project:
                for part in range(n_parts):
                    project(rows, part)
            if do_finish:
                finish(rows)
            if do_mixers:
                cat_old[rows, :] = cat_scr[rows, :]
            if do_project:
                h_scr[rows, :] = h_new[rows, :]
            return carry
        lax.fori_loop(0, n_chunks, body, 0)
        if do_mixers:
            str_ref[...] = sret[...]
            sth_ref[...] = shg[...]

    pl.when(j == 0)(lambda: run(True, False, False))
    pl.when(j == 1)(lambda: run(True, True, False))
    pl.when((j >= 2) & (j < n_tiles))(lambda: run(True, True, True))
    pl.when(j == n_tiles)(lambda: run(False, True, True))
    pl.when(j == n_tiles + 1)(lambda: run(False, False, True))


def _const_spec(shape, ngrid):
    zeros = (0,) * len(shape)
    if ngrid == 1:
        return pl.BlockSpec(shape, lambda i: zeros)
    return pl.BlockSpec(shape, lambda b, j: zeros)


def _prompt_mix(layer, x, w_in_b, w_out_b, sinks, lbraw, hgw, g1, b1, tabs, consts, w_up, w_down):
    t = PROMPT_TILE
    seq_tiles = SEQ // t
    wspec = lambda r, c: pl.BlockSpec((r, c), lambda j: (0, 0), pipeline_mode=pl.Buffered(1))
    vspec = lambda c: pl.BlockSpec((None, 1, c), lambda j: (layer, 0, 0))
    cast_slabs = BATCH * seq_tiles
    slab = lambda j: jnp.minimum(j, cast_slabs - 1)
    last = BATCH * seq_tiles - 1
    tile_new = lambda j: jnp.minimum(j, last)
    tile_mix = lambda j: jnp.clip(j - 1, 0, last)
    tile_old = lambda j: jnp.clip(j - 2, 0, last)
    tspec = pl.BlockSpec((t, LANES), lambda j: (tile_mix(j) % seq_tiles, 0))
    sspec = lambda r, c: pl.BlockSpec((None, r, c), lambda j: (tile_mix(j) // seq_tiles, 0, 0))
    in_specs = [pl.BlockSpec(memory_space=pltpu.SMEM),
                pl.BlockSpec((t, D_MODEL), lambda j: (tile_new(j), 0)),
                pl.BlockSpec((t, D_MODEL), lambda j: (tile_old(j), 0)),
                wspec(D_MODEL, IN_COLS), wspec(D_MODEL, D_MODEL),
                _const_spec((DEPTH, 256), 1), vspec(256), vspec(D_MODEL), vspec(D_MODEL),
                tspec, tspec, tspec, tspec] + [_const_spec(c.shape, 1) for c in consts]
    up_rows, dn_rows = D_MODEL // cast_slabs, D_FF // cast_slabs
    in_specs += [pl.BlockSpec((None, up_rows, D_FF), lambda j: (layer, slab(j), 0)),
                 pl.BlockSpec((None, dn_rows, D_MODEL), lambda j: (layer, slab(j), 0))]
    out_shape = [jax.ShapeDtypeStruct((BATCH * SEQ, D_MODEL), F32),
                 jax.ShapeDtypeStruct((BATCH, 128, 256), F32),
                 jax.ShapeDtypeStruct((BATCH, 256, 256), F32),
                 jax.ShapeDtypeStruct((BATCH, WINDOW, 128), F32),
                 jax.ShapeDtypeStruct((BATCH, WINDOW, 128), F32),
                 jax.ShapeDtypeStruct((D_MODEL, D_FF), BF16),
                 jax.ShapeDtypeStruct((D_FF, D_MODEL), BF16)]
    out_specs = [pl.BlockSpec((t, D_MODEL), lambda j: (tile_old(j), 0)),
                 sspec(128, 256), sspec(256, 256), sspec(WINDOW, 128), sspec(WINDOW, 128),
                 pl.BlockSpec((up_rows, D_FF), lambda j: (slab(j), 0)),
                 pl.BlockSpec((dn_rows, D_MODEL), lambda j: (slab(j), 0))]
    scratch = [pltpu.VMEM((t, IN_COLS), F32), pltpu.VMEM((t, IN_COLS), F32),
               pltpu.VMEM((t, D_MODEL), BF16), pltpu.VMEM((t, D_MODEL), BF16),
               pltpu.VMEM((128, 256), F32), pltpu.VMEM((256, 256), F32),
               pltpu.VMEM((WINDOW, 128), BF16), pltpu.VMEM((WINDOW, 128), BF16),
               pltpu.VMEM((PROMPT_LEVELS + 1, CHUNK, 256), BF16),
               pltpu.VMEM((PROMPT_LEVELS + 1, CHUNK, 256), BF16),
               pltpu.VMEM((CHUNK, 4 * CHUNK), BF16), pltpu.VMEM((CHUNK, 4 * CHUNK), BF16),
               pltpu.VMEM((CHUNK, 256), BF16), pltpu.VMEM((CHUNK, 256), F32),
               pltpu.VMEM((8, 256), F32)]
    return pl.pallas_call(
        functools.partial(_prompt_mix_kernel, layer=layer),
        grid=(BATCH * seq_tiles + 2,), in_specs=in_specs, out_specs=out_specs,
        out_shape=out_shape, scratch_shapes=scratch, name=f"prompt_mix_{layer}",
        compiler_params=pltpu.CompilerParams(dimension_semantics=("arbitrary",),
                                             vmem_limit_bytes=VMEM_LIMIT),
    )(sinks, x, x, w_in_b, w_out_b, lbraw, hgw, g1, b1, *tabs, *consts, w_up, w_down)


def _ffn_kernel(x_ref, wup_ref, wdn_ref, g_ref, b_ref, *rest):
    if len(rest) == 1:
        (o_ref,) = rest
    else:
        win_f32, wout_f32, o_ref, win_b, wout_b = rest
        win_b[...] = win_f32[...].astype(BF16)
        wout_b[...] = wout_f32[...].astype(BF16)
    x = x_ref[...]
    xb = x.astype(BF16)
    acc = jnp.zeros(x.shape, F32)
    step = 1024
    for i in range(D_FF // step):
        hid = jnp.maximum(jnp.dot(xb, wup_ref[:, i * step:(i + 1) * step],
                                  preferred_element_type=F32), 0.0)
        acc = acc + jnp.dot((hid * hid).astype(BF16), wdn_ref[i * step:(i + 1) * step, :],
                            preferred_element_type=F32)
    o_ref[...] = _layernorm(ALPHA * x + acc, g_ref[...], b_ref[...])


def _ffn(layer, x, w_up_b, w_down_b, g2, b2, next_mixer_weights=None):
    rows = x.shape[0]
    t = min(rows, FFN_TILE)
    steps = rows // t
    in_specs = [pl.BlockSpec((t, D_MODEL), lambda i: (i, 0)),
                pl.BlockSpec((D_MODEL, D_FF), lambda i: (0, 0), pipeline_mode=pl.Buffered(1)),
                pl.BlockSpec((D_FF, D_MODEL), lambda i: (0, 0), pipeline_mode=pl.Buffered(1)),
                pl.BlockSpec((None, 1, D_MODEL), lambda i: (layer, 0, 0)),
                pl.BlockSpec((None, 1, D_MODEL), lambda i: (layer, 0, 0))]
    out_specs = [pl.BlockSpec((t, D_MODEL), lambda i: (i, 0))]
    out_shape = [jax.ShapeDtypeStruct((rows, D_MODEL), F32)]
    args = [x, w_up_b, w_down_b, g2, b2]
    if next_mixer_weights is not None:
        slab = D_MODEL // steps
        for w in next_mixer_weights:
            cols = w.shape[2]
            in_specs.append(pl.BlockSpec((None, slab, cols), lambda i: (layer + 1, i, 0)))
            out_specs.append(pl.BlockSpec((slab, cols), lambda i: (i, 0)))
            out_shape.append(jax.ShapeDtypeStruct((D_MODEL, cols), BF16))
            args.append(w)
    return pl.pallas_call(
        _ffn_kernel, grid=(steps,), in_specs=in_specs, out_specs=out_specs, out_shape=out_shape,
        name=f"ffn_{layer}_{rows}",
        compiler_params=pltpu.CompilerParams(dimension_semantics=("arbitrary",),
                                             vmem_limit_bytes=VMEM_LIMIT),
    )(*args)


def _sample_mix_kernel(sinks_ref, x_ref, win_ref, wout_ref, lbraw_ref, hgw_ref, g1_ref, b1_ref,
                       cr_ref, sr_ref, cs_ref, ss_ref, sret_ref, shg_ref, ck_ref, cv_ref,
                       dec_ref, inter_ref, kdec_ref, mall_ref, hblk_ref,
                       nsret_all, nshg_all, nk_all, nv_all,
                       x1_ref, nsret_ref, nshg_ref, nk_ref, nv_ref, os_scr, *, layer, gam4):
    del nsret_all, nshg_all, nk_all, nv_all
    rows, seqs = SAMPLE_ROWS, SAMPLE_SEQS
    x = x_ref[...]
    h = jnp.dot(x.astype(BF16), win_ref[...], preferred_element_type=F32)
    cr, sr, cs, ss = cr_ref[...], sr_ref[...], cs_ref[...], ss_ref[...]
    lb = _hg_lower_bound(lbraw_ref, layer)

    q = _rope(h[:, C_RQ:C_RQ + 128], cr, sr, RET_QK // 2, RET_QK)
    k = _rope(h[:, C_RK:C_RK + 128], cr, sr, RET_QK // 2, RET_QK) * (RET_QK ** -0.5)
    v = h[:, C_RV:C_RV + 256]
    o_r = _ret_intra(q, k, v, dec_ref, rows)
    hk = RET_HEADS * RET_QK
    s0 = sret_ref[...].reshape(seqs * hk, RET_V)
    r_i = (_iota((seqs * hk, 256), 0) % hk) // RET_QK
    c_i = _iota((seqs * hk, 256), 1) // RET_V
    s0bd = jnp.where(r_i == c_i, jnp.tile(s0, (1, 4)), 0.0).astype(BF16)
    own = _iota((rows, seqs * hk), 0) // DEC_SEQ == _iota((rows, seqs * hk), 1) // hk
    o_r = o_r + _mm(jnp.where(own, jnp.tile(q, (1, seqs)), 0.0), s0bd) * inter_ref[...]
    kdt = (k * kdec_ref[...]).T
    col_seq = _iota((hk, rows), 1) // DEC_SEQ
    for n in range(seqs):
        upd = _mm(jnp.where(col_seq == n, kdt, 0.0), v)
        for hd in range(RET_HEADS):
            nsret_ref[n, hd] = (gam4[hd] * sret_ref[n, hd]
                                + upd[RET_QK * hd:RET_QK * (hd + 1), RET_V * hd:RET_V * (hd + 1)])
    ret = _rms_gate(o_r, h[:, C_RG:C_RG + 256], hblk_ref)

    q = h[:, C_HQ:C_HQ + 256]
    lf = _hg_log_forget(h[:, C_HF:C_HF + 256], lb)
    k = 1.0 - jnp.exp(lf)
    v = h[:, C_HI:C_HI + 256]
    o_h, b, tail = _hgrn_intra(q, k, v, lf, mall_ref, hblk_ref, rows, SAMPLE_LEVELS)
    hk = HG_HEADS * HG_K
    s0 = shg_ref[...].reshape(seqs * hk, HG_V)
    r_i = (_iota((seqs * hk, 256), 0) % hk) // HG_K
    c_i = _iota((seqs * hk, 256), 1) // HG_V
    s0bd = jnp.where(r_i == c_i, jnp.tile(s0, (1, 4)), 0.0).astype(BF16)
    own = _iota((rows, seqs * hk), 0) // DEC_SEQ == _iota((rows, seqs * hk), 1) // hk
    o_h = o_h + _mm(jnp.where(own, jnp.tile(q * jnp.exp(b), (1, seqs)), 0.0), s0bd)
    kdt = (k * jnp.exp(tail)).T
    ebt = jnp.exp(b).T
    col_seq = _iota((hk, rows), 1) // DEC_SEQ
    for n in range(seqs):
        upd = _mm(jnp.where(col_seq == n, kdt, 0.0), v)
        last = DEC_SEQ * n + DEC_SEQ - 1
        dcol = ebt[:, last:last + 1]
        for hd in range(HG_HEADS):
            nshg_ref[n, hd] = (dcol[HG_K * hd:HG_K * (hd + 1)] * shg_ref[n, hd]
                               + upd[HG_K * hd:HG_K * (hd + 1), HG_V * hd:HG_V * (hd + 1)])
    hg = _rms_gate(o_h, h[:, C_HG:C_HG + 256], hblk_ref, hgw_ref[...])

    k = _rope(h[:, C_SK:C_SK + 128], cs, ss, ROT_DIM // 2, SWA_HD)
    v = h[:, C_SV:C_SV + 128]
    kb, vb = k.astype(BF16), v.astype(BF16)
    qm = _swa_heads(h[:, C_SQ:C_SQ + 512], lambda t: _rope(t, cs, ss, ROT_DIM // 2, SWA_HD))
    r64 = _iota((64, 1), 0)
    sinkcol = jnp.zeros((64, 1), F32)
    for hd in range(SWA_HEADS):
        sinkcol = jnp.where(r64 // 8 == hd, sinks_ref[layer, hd], sinkcol)
    rc = _iota((64, 128), 0)
    cc = _iota((64, 128), 1)
    first_seq = (rc % 8) < DEC_SEQ
    cmask = cc >= rc % DEC_SEQ
    rn = _iota((64, rows), 0)
    cn = _iota((64, rows), 1)
    new_lanes = _iota((128, 128), 1) >= WINDOW - DEC_SEQ
    pad_rows = jnp.zeros((WINDOW - 8, 128), F32)
    for pr in range(seqs // 2):
        na, nb = 2 * pr, 2 * pr + 1
        qp = jnp.concatenate([t[8 * pr:8 * pr + 8] for t in qm], axis=0)
        ka, kb_c = ck_ref[na], ck_ref[nb]
        va, vb_c = cv_ref[na], cv_ref[nb]
        sc = jnp.where(first_seq, _mm(qp, ka), _mm(qp, kb_c))
        sc = jnp.where(cmask, sc, NEG)
        nmask = (cn // DEC_SEQ == 2 * pr + (rn % 8) // DEC_SEQ) & (cn % DEC_SEQ <= rn % DEC_SEQ)
        sn = jnp.where(nmask, _mm_nt(qp, kb), NEG)
        m = jnp.maximum(jnp.maximum(jnp.max(sc, axis=1, keepdims=True),
                                    jnp.max(sn, axis=1, keepdims=True)), sinkcol)
        pc = jnp.exp(sc - m)
        pn = jnp.exp(sn - m)
        den = (jnp.sum(pc, axis=1, keepdims=True) + jnp.sum(pn, axis=1, keepdims=True)
               + jnp.exp(sinkcol - m))
        o = (_mm_nt(jnp.where(first_seq, pc, 0.0), va) + _mm_nt(jnp.where(first_seq, 0.0, pc), vb_c)
             + _mm(pn, vb)) / den
        pieces = []
        for hd in range(SWA_HEADS):
            g = hd // SWA_GROUP
            pieces.append(o[8 * hd:8 * hd + 8, SWA_HD * g:SWA_HD * (g + 1)])
        os_scr[8 * pr:8 * pr + 8, :] = jnp.concatenate(pieces, axis=1)
        for new, old_a, old_b, dst in ((k[8 * pr:8 * pr + 8], ka, kb_c, nk_ref),
                                       (v[8 * pr:8 * pr + 8], va, vb_c, nv_ref)):
            new_t = jnp.concatenate([new, pad_rows], axis=0).T
            dst[na] = jnp.where(new_lanes, pltpu.roll(new_t, WINDOW - DEC_SEQ, 1),
                                pltpu.roll(old_a, WINDOW - DEC_SEQ, 1))
            dst[nb] = jnp.where(new_lanes, pltpu.roll(new_t, WINDOW - 2 * DEC_SEQ, 1),
                                pltpu.roll(old_b, WINDOW - DEC_SEQ, 1))

    cat = jnp.concatenate([ret, hg, os_scr[...]], axis=1).astype(BF16)
    mix = jnp.dot(cat, wout_ref[...], preferred_element_type=F32)
    x1_ref[...] = _layernorm(ALPHA * x + mix, g1_ref[...], b1_ref[...])


def _sample_mix(layer, x, w_in, w_out, sinks, lbraw, hgw, g1, b1, tabs, state_ret, state_hgrn,
                cache_k, cache_v, consts, gam4, stacked):
    r, s = SAMPLE_ROWS, SAMPLE_SEQS
    wspec = lambda a, c: pl.BlockSpec((a, c), lambda i: (0, 0), pipeline_mode=pl.Buffered(1))
    vspec = lambda c: pl.BlockSpec((None, 1, c), lambda i: (layer, 0, 0))
    in_specs = [pl.BlockSpec(memory_space=pltpu.SMEM),
                pl.BlockSpec((r, D_MODEL), lambda i: (i, 0)),
                wspec(D_MODEL, IN_COLS), wspec(D_MODEL, D_MODEL),
                _const_spec((DEPTH, 256), 1), vspec(256), vspec(D_MODEL), vspec(D_MODEL)]
    in_specs += [_const_spec((r, LANES), 1)] * 4
    in_specs += [pl.BlockSpec((None, s, RET_HEADS, RET_QK, RET_V), lambda i: (layer, i, 0, 0, 0)),
                 pl.BlockSpec((None, s, HG_HEADS, HG_K, HG_V), lambda i: (layer, i, 0, 0, 0)),
                 pl.BlockSpec((None, s, WINDOW, 128), lambda i: (layer, i, 0, 0)),
                 pl.BlockSpec((None, s, WINDOW, 128), lambda i: (layer, i, 0, 0))]
    in_specs += [_const_spec(c.shape, 1) for c in consts]
    first_stacked = len(in_specs)
    in_specs += [pl.BlockSpec(memory_space=pl.ANY)] * len(stacked)
    out_shape = [jax.ShapeDtypeStruct((DEC_BATCH * DEC_SEQ, D_MODEL), F32)]
    out_shape += [jax.ShapeDtypeStruct(a.shape, a.dtype) for a in stacked]
    out_specs = [pl.BlockSpec((r, D_MODEL), lambda i: (i, 0)),
                 pl.BlockSpec((None, s, RET_HEADS, RET_QK, RET_V), lambda i: (layer, i, 0, 0, 0)),
                 pl.BlockSpec((None, s, HG_HEADS, HG_K, HG_V), lambda i: (layer, i, 0, 0, 0)),
                 pl.BlockSpec((None, s, WINDOW, 128), lambda i: (layer, i, 0, 0)),
                 pl.BlockSpec((None, s, WINDOW, 128), lambda i: (layer, i, 0, 0))]
    return pl.pallas_call(
        functools.partial(_sample_mix_kernel, layer=layer, gam4=gam4),
        grid=(DEC_BATCH // s,), in_specs=in_specs, out_specs=out_specs, out_shape=out_shape,
        input_output_aliases={first_stacked + i: 1 + i for i in range(len(stacked))},
        scratch_shapes=[pltpu.VMEM((r, 512), F32)], name=f"sample_mix_{layer}",
        compiler_params=pltpu.CompilerParams(dimension_semantics=("arbitrary",),
                                             vmem_limit_bytes=VMEM_LIMIT),
    )(sinks, x, w_in, w_out, lbraw, hgw, g1, b1, *tabs, state_ret, state_hgrn, cache_k, cache_v,
      *consts, *stacked)


def _decay_levels(rows, seqlen, levels):
    t = np.arange(rows)[:, None]
    u = np.arange(rows)[None, :]
    same = (t // seqlen) == (u // seqlen)
    blocks = [same & (u <= t)]
    for l in range(levels):
        m = 1 << l
        r = (t // (2 * m)) * (2 * m) + m
        lower = t >= r
        blocks.append(np.where(lower, (u >= r) & (u <= t), (u > t) & (u <= r - 1)))
    blocks.append(same & (u > t))
    return jnp.asarray(np.concatenate(blocks, axis=0).astype(np.float32), dtype=BF16)


def _rope_tables(pos):
    pos = pos.astype(F32)[:, None]
    inv = RET_THETA ** (-jnp.linspace(0.0, 1.0, RET_QK // 2, dtype=F32))
    ang = pos * inv[None, :]
    cr = jnp.tile(jnp.concatenate([jnp.cos(ang), jnp.cos(ang)], -1), (1, RET_HEADS))
    sr = jnp.tile(jnp.concatenate([-jnp.sin(ang), jnp.sin(ang)], -1), (1, RET_HEADS))
    inv = ROPE_THETA ** (-jnp.arange(0, ROT_DIM, 2, dtype=F32) / ROT_DIM)
    ang = pos * inv[None, :]
    rest = SWA_HD - ROT_DIM
    ones = jnp.ones((ang.shape[0], rest), F32)
    zeros = jnp.zeros((ang.shape[0], rest), F32)
    cs = jnp.tile(jnp.concatenate([jnp.cos(ang), jnp.cos(ang), ones], -1), (1, 2))
    ss = jnp.tile(jnp.concatenate([-jnp.sin(ang), jnp.sin(ang), zeros], -1), (1, 2))
    return cr, sr, cs, ss


def _ret_tables(tok, same):
    lg = jnp.log(1.0 - 2.0 ** (-5.0 - jnp.arange(RET_HEADS, dtype=F32)))
    n_tok = int(tok.max()) + 1
    tokf = jnp.asarray(tok, F32)
    diff = tokf[:, None] - tokf[None, :]
    dec = jnp.where(jnp.asarray(same) & (diff >= 0),
                    jnp.exp(jnp.maximum(diff, 0.0)[None] * lg[:, None, None]), 0.0)
    inter = jnp.repeat(jnp.exp((tokf + 1.0)[:, None] * lg[None, :]), RET_V, axis=1)
    kdec = jnp.repeat(jnp.exp((n_tok - 1.0 - tokf)[:, None] * lg[None, :]), RET_QK, axis=1)
    total = jnp.exp(n_tok * lg)
    return dec, inter, kdec, total


def kernel(x_prompt, x_sample, state_ret, state_hgrn, cache_swa_k, cache_swa_v, w_in, w_out,
           swa_sinks, hg_lb_raw, hg_norm_w, ln_mix_g, ln_mix_b, w_up, w_down, ln_ffn_g, ln_ffn_b):
    w_in_b, w_out_b = w_in[0].astype(BF16), w_out[0].astype(BF16)
    hgw = jnp.tile(hg_norm_w, (1, HG_HEADS)).reshape(DEPTH, 1, HG_HEADS * HG_V)
    g1, b1 = ln_mix_g.reshape(DEPTH, 1, D_MODEL), ln_mix_b.reshape(DEPTH, 1, D_MODEL)
    g2, b2 = ln_ffn_g.reshape(DEPTH, 1, D_MODEL), ln_ffn_b.reshape(DEPTH, 1, D_MODEL)
    cache_k = cache_swa_k.transpose(0, 1, 3, 4, 2).reshape(DEPTH, DEC_BATCH, 128, WINDOW)
    cache_v = cache_swa_v.transpose(0, 1, 3, 4, 2).reshape(DEPTH, DEC_BATCH, 128, WINDOW)

    hblk = jnp.asarray((np.arange(256)[:, None] // 64 == np.arange(256)[None, :] // 64)
                       .astype(np.float32), dtype=BF16)
    p_tabs = _rope_tables(jnp.arange(SEQ))
    tok = np.arange(CHUNK)
    dec, inter, kdec, total = _ret_tables(tok, np.ones((CHUNK, CHUNK), bool))
    rowdec = jnp.broadcast_to(jnp.repeat(total, RET_QK)[:, None], (128, 256))
    t_i, s_i = np.arange(CHUNK)[:, None], np.arange(CHUNK)[None, :]
    tri = jnp.asarray((s_i <= t_i).astype(np.float32), dtype=BF16)
    lvl = np.where(s_i < t_i, np.floor(np.log2(np.maximum(t_i ^ s_i, 1))),
                   np.where(s_i == t_i, PROMPT_LEVELS, -1)).astype(np.int32)
    q_i, k_j = np.arange(CHUNK)[:, None], np.arange(2 * CHUNK)[None, :]
    band = np.where((k_j >= q_i) & (k_j <= q_i + WINDOW), np.where(k_j >= WINDOW, 2, 1), 0)
    p_consts = (dec, inter, kdec, rowdec, tri, hblk, jnp.asarray(lvl),
                jnp.asarray(band.astype(np.int32)))
    row = np.arange(SAMPLE_ROWS)
    s_tabs = _rope_tables(PAST_LEN + jnp.asarray(row % DEC_SEQ))
    same = (row[:, None] // DEC_SEQ) == (row[None, :] // DEC_SEQ)
    dec_s, inter_s, kdec_s, total_s = _ret_tables(row % DEC_SEQ, same)
    s_consts = (dec_s, inter_s, kdec_s, _decay_levels(SAMPLE_ROWS, DEC_SEQ, SAMPLE_LEVELS), hblk)
    gam4 = tuple(float((1.0 - 2.0 ** (-5.0 - h)) ** DEC_SEQ) for h in range(RET_HEADS))
    del total_s

    xp = x_prompt.reshape(BATCH * SEQ, D_MODEL)
    xs = x_sample.reshape(DEC_BATCH * DEC_SEQ, D_MODEL)
    outs = [[] for _ in range(4)]
    stacked = (jnp.zeros((DEPTH, DEC_BATCH, RET_HEADS, RET_QK, RET_V), F32),
               jnp.zeros((DEPTH, DEC_BATCH, HG_HEADS, HG_K, HG_V), F32),
               jnp.zeros((DEPTH, DEC_BATCH, 128, WINDOW), F32),
               jnp.zeros((DEPTH, DEC_BATCH, 128, WINDOW), F32))
    for l in range(DEPTH):
        x1, st_r, st_h, pk, pv, w_up_b, w_down_b = _prompt_mix(
            l, xp, w_in_b, w_out_b, swa_sinks, hg_lb_raw, hgw, g1, b1, p_tabs, p_consts,
            w_up, w_down)
        ffn_out = _ffn(l, x1, w_up_b, w_down_b, g2, b2,
                       (w_in, w_out) if l + 1 < DEPTH else None)
        xp = ffn_out[0]
        outs[0].append(jnp.stack([st_r[:, RET_QK * h:RET_QK * (h + 1), RET_V * h:RET_V * (h + 1)]
                                  for h in range(RET_HEADS)], axis=1))
        outs[1].append(jnp.stack([st_h[:, HG_K * h:HG_K * (h + 1), HG_V * h:HG_V * (h + 1)]
                                  for h in range(HG_HEADS)], axis=1))
        outs[2].append(pk.reshape(BATCH, SWA_KV, SWA_HD, WINDOW).transpose(0, 3, 1, 2))
        outs[3].append(pv.reshape(BATCH, SWA_KV, SWA_HD, WINDOW).transpose(0, 3, 1, 2))

        x1, *stacked = _sample_mix(l, xs, w_in_b, w_out_b, swa_sinks, hg_lb_raw, hgw, g1, b1,
                                   s_tabs, state_ret, state_hgrn, cache_k, cache_v, s_consts,
                                   gam4, tuple(stacked))
        xs = _ffn(l, x1, w_up_b, w_down_b, g2, b2)[0]
        if l + 1 < DEPTH:
            w_in_b, w_out_b = ffn_out[1], ffn_out[2]

    ns_r, ns_h, nk, nv = stacked
    nk = nk.reshape(DEPTH, DEC_BATCH, SWA_KV, SWA_HD, WINDOW).transpose(0, 1, 4, 2, 3)
    nv = nv.reshape(DEPTH, DEC_BATCH, SWA_KV, SWA_HD, WINDOW).transpose(0, 1, 4, 2, 3)
    return ((xp.reshape(BATCH, SEQ, D_MODEL), xs.reshape(DEC_BATCH, DEC_SEQ, D_MODEL))
            + tuple(jnp.stack(o) for o in outs) + (ns_r, ns_h, nk, nv))
```

```python
import functools

import numpy as np
import jax
import jax.numpy as jnp
from jax import lax
from jax.experimental import pallas as pl
from jax.experimental.pallas import tpu as pltpu

F32 = jnp.float32
BF16 = jnp.bfloat16

D_MODEL = 1024
BATCH = 2
SEQ = 8192
DEPTH = 4
DEC_BATCH = 128
DEC_SEQ = 4
PAST_LEN = 8192
RET_HEADS = 4
RET_QK = 32
RET_V = 64
RET_THETA = 10000.0
HG_HEADS = 4
HG_K = 64
HG_V = 64
SWA_HEADS = 8
SWA_KV = 2
SWA_GROUP = SWA_HEADS // SWA_KV
SWA_HD = 64
WINDOW = 128
ROT_DIM = 16
ROPE_THETA = 500000.0
CHUNK = 128
D_FF = 4 * D_MODEL
IN_COLS = 2560
ALPHA = (2.0 * DEPTH) ** 0.25
LN_EPS = 1e-5
RMS_EPS = 1e-6
NEG = -1e30

C_RQ, C_RK, C_RV, C_RG = 0, 128, 256, 512
C_HQ, C_HF, C_HI, C_HG = 768, 1024, 1280, 1536
C_SQ, C_SK, C_SV = 1792, 2304, 2432

LANES = 128
PROMPT_TILE = 512
FFN_TILE = 512
SAMPLE_SEQS = 16
SAMPLE_ROWS = SAMPLE_SEQS * DEC_SEQ
PROMPT_LEVELS = 7
SAMPLE_LEVELS = 2
VMEM_LIMIT = 56 * 1024 * 1024


def _mm(a, b):
    return jnp.dot(a.astype(BF16), b.astype(BF16), preferred_element_type=F32)


def _mm_nt(a, b):
    return lax.dot_general(a.astype(BF16), b.astype(BF16), (((1,), (1,)), ((), ())),
                           preferred_element_type=F32)


def _iota(shape, dim):
    return lax.broadcasted_iota(jnp.int32, shape, dim)


def _layernorm(x, g, b):
    mu = jnp.mean(x, axis=-1, keepdims=True)
    xc = x - mu
    var = jnp.mean(xc * xc, axis=-1, keepdims=True)
    return xc * lax.rsqrt(var + LN_EPS) * g + b


def _rope(x, cos_t, sin_t, half, period):
    first_half = (_iota((1, LANES), 1) % period) < half
    fwd = pltpu.roll(x, LANES - half, 1)
    bwd = pltpu.roll(x, half, 1)
    return x * cos_t + jnp.where(first_half, fwd, bwd) * sin_t


def _silu(x):
    return x * (1.0 / (1.0 + jnp.exp(-x)))


def _split3(x):
    hi = x.astype(BF16)
    r = x - hi.astype(F32)
    mid = r.astype(BF16)
    lo = (r - mid.astype(F32)).astype(BF16)
    return hi, mid, lo


def _hg_lower_bound(lbraw_ref, layer):
    raw = lbraw_ref[...]
    e = jnp.exp(raw - jnp.max(raw, axis=0, keepdims=True))
    sm = e / jnp.sum(e, axis=0, keepdims=True)
    cum = sm[0:1]
    for i in range(1, layer + 1):
        cum = cum + sm[i:i + 1]
    return cum - sm[0:1]


def _hg_log_forget(hf, lb):
    logsig = jnp.minimum(hf, 0.0) - jnp.log(1.0 + jnp.exp(-jnp.abs(hf)))
    a = jnp.log(lb)
    b = jnp.log(1.0 - lb) + logsig
    return jnp.maximum(a, b) + jnp.log(1.0 + jnp.exp(-jnp.abs(a - b)))


def _level_exponents(b, rows, levels):
    g = rows // 8
    n = b.shape[1]
    b3 = b.reshape(g, 8, n)
    sub = _iota((g, 8, n), 1)
    dn1 = pltpu.roll(b3, 1, 1)
    dn2 = pltpu.roll(b3, 2, 1)
    up1 = pltpu.roll(b3, 7, 1)
    out = [jnp.where(sub % 2 == 1, b3 - dn1, 0.0)]
    m4 = sub % 4
    out.append(jnp.where(m4 == 0, up1 - b3,
                         jnp.where(m4 == 1, 0.0, jnp.where(m4 == 2, b3 - dn1, b3 - dn2))))
    if levels > 2:
        r3 = jnp.broadcast_to(b3[:, 3:4, :], b3.shape)
        out.append(jnp.where(sub < 4, r3 - b3, b3 - r3))
        r7 = jnp.broadcast_to(b3[:, 7:8, :], b3.shape)
        grp = _iota((g, 8, n), 0)
        for l in range(3, levels):
            span = (2 << l) // 8
            half = span // 2
            ref = r7.reshape(g // span, span, 8, n)[:, half - 1:half]
            ref = jnp.broadcast_to(ref, (g // span, span, 8, n)).reshape(g, 8, n)
            out.append(jnp.where(grp % span >= half, b3 - ref, ref - b3))
    return [e.reshape(rows, n) for e in out]


def _head_blockdiag(v, rows):
    r = _iota((4 * rows, 256), 0) // rows
    c = _iota((4 * rows, 256), 1) // 64
    return jnp.where(r == c, jnp.tile(v, (4, 1)), 0.0).astype(BF16)


def _rms_gate(o, gate, hblk_ref, w=None):
    ms = _mm(o * o, hblk_ref[...]) * (1.0 / 64.0)
    y = o * lax.rsqrt(ms + RMS_EPS)
    if w is not None:
        y = y * w
    return y * _silu(gate)


def _ret_intra(q, k, v, dec_ref, rows):
    lane_head = _iota((rows, 128), 1) // RET_QK
    a = [_mm_nt(jnp.where(lane_head == h, q, 0.0), k) * dec_ref[h] for h in range(RET_HEADS)]
    return _mm(jnp.concatenate(a, axis=1), _head_blockdiag(v, rows))


def _hgrn_intra(q, k, v, lf, mall_ref, hblk_ref, rows, levels):
    l3 = jnp.concatenate(_split3(lf), axis=1)
    eb = jnp.dot(mall_ref[...], l3, preferred_element_type=F32)
    eb = eb[:, 0:256] + eb[:, 256:512] + eb[:, 512:768]
    b = eb[0:rows]
    tail = eb[(levels + 1) * rows:(levels + 2) * rows]
    row = _iota((rows, 256), 0)
    lane_head = _iota((rows, 256), 1) // HG_K
    ra = _iota((rows, rows), 0)
    ca = _iota((rows, rows), 1)
    acc = [jnp.zeros((rows, rows), F32) for _ in range(HG_HEADS)]
    for l in range(levels):
        w = jnp.exp(eb[(l + 1) * rows:(l + 2) * rows])
        lower = ((row >> l) & 1) == 1
        qs = jnp.where(lower, q * w, 0.0).astype(BF16)
        ks = jnp.where(lower, 0.0, k * w).astype(BF16)
        same_block = (ra >> (l + 1)) == (ca >> (l + 1))
        for h in range(HG_HEADS):
            p = _mm_nt(jnp.where(lane_head == h, qs, jnp.zeros_like(qs)), ks)
            acc[h] = acc[h] + jnp.where(same_block, p, 0.0)
    o = _mm(jnp.concatenate(acc, axis=1), _head_blockdiag(v, rows))
    o = o + _mm(q * k, hblk_ref[...]) * v
    return o, b, tail


def _swa_heads(qcols, rope_q, scale=SWA_HD ** -0.5):
    out = []
    for c in range(4):
        qc = rope_q(qcols[:, LANES * c:LANES * (c + 1)])
        if scale is not None:
            qc = qc * scale
        half = _iota(qc.shape, 1) // SWA_HD
        for p in range(2):
            g = (2 * c + p) // SWA_GROUP
            qm = jnp.where(half == p, qc, 0.0)
            if p != g:
                qm = pltpu.roll(qm, SWA_HD, 1)
            out.append(qm)
    return out


def _prompt_mix_kernel(sinks_ref, x_ref, xold_ref, win_ref, wout_ref, lbraw_ref, hgw_ref, g1_ref,
                       b1_ref, cr_ref, sr_ref, cs_ref, ss_ref, dec_ref, inter_ref, kdec_ref,
                       rowdec_ref, tri_ref, hblk_ref, lvl_ref, band_ref, wup_f32, wdn_f32,
                       x1_ref, str_ref, sth_ref, pk_ref, pv_ref, wup_b, wdn_b,
                       h_new, h_scr, cat_scr, cat_old, sret, shg, kprev, vprev, qs_scr, ks_scr,
                       ar_scr, ah_scr, qb_scr, kb_scr, bl_scr, *, layer):
    j = pl.program_id(0)
    seq_tiles = SEQ // PROMPT_TILE
    n_tiles = BATCH * seq_tiles
    n_chunks = PROMPT_TILE // CHUNK
    levels = PROMPT_LEVELS
    mix_tile = (j - 1) % seq_tiles

    wup_b[...] = wup_f32[...].astype(BF16)
    wdn_b[...] = wdn_f32[...].astype(BF16)

    @pl.when(mix_tile == 0)
    def _():
        sret[...] = jnp.zeros_like(sret)
        shg[...] = jnp.zeros_like(shg)
        kprev[...] = jnp.zeros_like(kprev)
        vprev[...] = jnp.zeros_like(vprev)

    n_parts = 5
    part_cols = IN_COLS // n_parts

    def project(rows, part):
        cols = slice(part * part_cols, (part + 1) * part_cols)
        h_new[rows, cols] = jnp.dot(x_ref[rows, :].astype(BF16), win_ref[:, cols],
                                    preferred_element_type=F32)

    def finish(rows):
        mix = jnp.dot(cat_old[rows, :], wout_ref[...], preferred_element_type=F32)
        x1_ref[rows, :] = _layernorm(ALPHA * xold_ref[rows, :] + mix, g1_ref[...], b1_ref[...])

    def mixers(c, rows, filler):
        lb = _hg_lower_bound(lbraw_ref, layer)
        cr, sr = cr_ref[rows, :], sr_ref[rows, :]
        cs, ss = cs_ref[rows, :], ss_ref[rows, :]

        q = _rope(h_scr[rows, C_RQ:C_RQ + 128], cr, sr, RET_QK // 2, RET_QK)
        k = _rope(h_scr[rows, C_RK:C_RK + 128], cr, sr, RET_QK // 2, RET_QK) * (RET_QK ** -0.5)
        v = h_scr[rows, C_RV:C_RV + 256]
        lane_head = _iota((CHUNK, 128), 1) // RET_QK
        for hd in range(RET_HEADS):
            a = _mm_nt(jnp.where(lane_head == hd, q, 0.0), k) * dec_ref[hd]
            ar_scr[:, CHUNK * hd:CHUNK * (hd + 1)] = a.astype(BF16)
        o_r = jnp.dot(ar_scr[...], _head_blockdiag(v, CHUNK), preferred_element_type=F32)
        s0 = sret[...]
        o_r = o_r + _mm(q, s0) * inter_ref[...]
        r_i = _iota((128, 256), 0) // RET_QK
        c_i = _iota((128, 256), 1) // RET_V
        upd = _mm((k * kdec_ref[...]).T, v)
        sret[...] = rowdec_ref[...] * s0 + jnp.where(r_i == c_i, upd, 0.0)
        cat_scr[rows, 0:256] = _rms_gate(o_r, h_scr[rows, C_RG:C_RG + 256], hblk_ref).astype(BF16)
        filler(0)

        q = h_scr[rows, C_HQ:C_HQ + 256]
        lf = _hg_log_forget(h_scr[rows, C_HF:C_HF + 256], lb)
        k = 1.0 - jnp.exp(lf)
        l3 = jnp.concatenate(_split3(lf), axis=1)
        b = jnp.dot(tri_ref[...], l3, preferred_element_type=F32)
        b = b[:, 0:256] + b[:, 256:512] + b[:, 512:768]
        for l, e in enumerate(_level_exponents(b, CHUNK, levels)):
            w = jnp.exp(e)
            qs_scr[l] = (q * w).astype(BF16)
            ks_scr[l] = (k * w).astype(BF16)
        qs_scr[levels] = q.astype(BF16)
        ks_scr[levels] = k.astype(BF16)
        bl = b[CHUNK - 1:CHUNK, :]
        qb_scr[...] = (q * jnp.exp(b)).astype(BF16)
        kb_scr[...] = k * jnp.exp(bl - b)
        bl_scr[...] = jnp.broadcast_to(jnp.exp(bl), (8, 256))
        v = h_scr[rows, C_HI:C_HI + 256]
        filler(1)
        lvl = lvl_ref[...]
        for hd in range(HG_HEADS):
            lanes = slice(HG_K * hd, HG_K * (hd + 1))
            a = jnp.zeros((CHUNK, CHUNK), F32)
            for l in range(levels + 1):
                p = lax.dot_general(qs_scr[l, :, lanes], ks_scr[l, :, lanes],
                                    (((1,), (1,)), ((), ())), preferred_element_type=F32)
                a = jnp.where(lvl == l, p, a)
            ah_scr[:, CHUNK * hd:CHUNK * (hd + 1)] = a.astype(BF16)
        o_h = jnp.dot(ah_scr[...], _head_blockdiag(v, CHUNK), preferred_element_type=F32)
        s0 = shg[...]
        o_h = o_h + jnp.dot(qb_scr[...], s0.astype(BF16), preferred_element_type=F32)
        decay_t = jnp.broadcast_to(bl_scr[0:1, :], (128, 256)).T
        r_i = _iota((256, 256), 0) // HG_K
        c_i = _iota((256, 256), 1) // HG_V
        upd = _mm(kb_scr[...].T, v)
        shg[...] = jnp.tile(decay_t, (1, 2)) * s0 + jnp.where(r_i == c_i, upd, 0.0)
        cat_scr[rows, 256:512] = _rms_gate(o_h, h_scr[rows, C_HG:C_HG + 256], hblk_ref,
                                           hgw_ref[...]).astype(BF16)
        filler(2)

        k = _rope(h_scr[rows, C_SK:C_SK + 128], cs, ss, ROT_DIM // 2, SWA_HD)
        v = h_scr[rows, C_SV:C_SV + 128]
        ks_b = (k * (SWA_HD ** -0.5)).astype(BF16)
        kk = jnp.concatenate([kprev[...], ks_b], axis=0)
        vv = jnp.concatenate([vprev[...], v.astype(BF16)], axis=0)
        mask = band_ref[...] > jnp.where(mix_tile * n_chunks + c > 0, 0, 1)
        qm = _swa_heads(h_scr[rows, C_SQ:C_SQ + 512],
                        lambda t: _rope(t, cs, ss, ROT_DIM // 2, SWA_HD), scale=None)
        outs = []
        for hd in range(SWA_HEADS):
            g = hd // SWA_GROUP
            sink = sinks_ref[layer, hd]
            s = jnp.where(mask, _mm_nt(qm[hd], kk), NEG)
            m = jnp.maximum(jnp.max(s, axis=1, keepdims=True), sink)
            p = jnp.exp(s - m)
            den = jnp.sum(p, axis=1, keepdims=True) + jnp.exp(sink - m)
            outs.append(_mm(p, vv)[:, SWA_HD * g:SWA_HD * (g + 1)] / den)
            if hd % 2 == 1:
                col = 512 + LANES * (hd // 2)
                cat_scr[rows, col:col + LANES] = jnp.concatenate(outs, axis=1).astype(BF16)
                outs = []
            if hd == SWA_HEADS // 2 - 1:
                filler(3)
        filler(4)
        kprev[...] = ks_b
        vprev[...] = v.astype(BF16)
        pk_ref[...] = k.T
        pv_ref[...] = v.T


    def run(do_project, do_mixers, do_finish):
        def body(c, carry):
            rows = pl.ds(pl.multiple_of(c * CHUNK, CHUNK), CHUNK)
            fill = (lambda part: project(rows, part)) if do_project else (lambda part: None)
            if do_mixers:
                mixers(c, rows, fill)
            elif do_project:
                for part in range(n_parts):
                    project(rows, part)
            if do_finish:
                finish(rows)
            if do_mixers:
                cat_old[rows, :] = cat_scr[rows, :]
            if do_project:
                h_scr[rows, :] = h_new[rows, :]
            return carry
        lax.fori_loop(0, n_chunks, body, 0)
        if do_mixers:
            str_ref[...] = sret[...]
            sth_ref[...] = shg[...]

    pl.when(j == 0)(lambda: run(True, False, False))
    pl.when(j == 1)(lambda: run(True, True, False))
    pl.when((j >= 2) & (j < n_tiles))(lambda: run(True, True, True))
    pl.when(j == n_tiles)(lambda: run(False, True, True))
    pl.when(j == n_tiles + 1)(lambda: run(False, False, True))


def _const_spec(shape, ngrid):
    zeros = (0,) * len(shape)
    if ngrid == 1:
        return pl.BlockSpec(shape, lambda i: zeros)
    return pl.BlockSpec(shape, lambda b, j: zeros)


def _prompt_mix(layer, x, w_in_b, w_out_b, sinks, lbraw, hgw, g1, b1, tabs, consts, w_up, w_down):
    t = PROMPT_TILE
    seq_tiles = SEQ // t
    wspec = lambda r, c: pl.BlockSpec((r, c), lambda j: (0, 0), pipeline_mode=pl.Buffered(1))
    vspec = lambda c: pl.BlockSpec((None, 1, c), lambda j: (layer, 0, 0))
    cast_slabs = BATCH * seq_tiles
    slab = lambda j: jnp.minimum(j, cast_slabs - 1)
    last = BATCH * seq_tiles - 1
    tile_new = lambda j: jnp.minimum(j, last)
    tile_mix = lambda j: jnp.clip(j - 1, 0, last)
    tile_old = lambda j: jnp.clip(j - 2, 0, last)
    tspec = pl.BlockSpec((t, LANES), lambda j: (tile_mix(j) % seq_tiles, 0))
    sspec = lambda r, c: pl.BlockSpec((None, r, c), lambda j: (tile_mix(j) // seq_tiles, 0, 0))
    in_specs = [pl.BlockSpec(memory_space=pltpu.SMEM),
                pl.BlockSpec((t, D_MODEL), lambda j: (tile_new(j), 0)),
                pl.BlockSpec((t, D_MODEL), lambda j: (tile_old(j), 0)),
                wspec(D_MODEL, IN_COLS), wspec(D_MODEL, D_MODEL),
                _const_spec((DEPTH, 256), 1), vspec(256), vspec(D_MODEL), vspec(D_MODEL),
                tspec, tspec, tspec, tspec] + [_const_spec(c.shape, 1) for c in consts]
    up_rows, dn_rows = D_MODEL // cast_slabs, D_FF // cast_slabs
    in_specs += [pl.BlockSpec((None, up_rows, D_FF), lambda j: (layer, slab(j), 0)),
                 pl.BlockSpec((None, dn_rows, D_MODEL), lambda j: (layer, slab(j), 0))]
    out_shape = [jax.ShapeDtypeStruct((BATCH * SEQ, D_MODEL), F32),
                 jax.ShapeDtypeStruct((BATCH, 128, 256), F32),
                 jax.ShapeDtypeStruct((BATCH, 256, 256), F32),
                 jax.ShapeDtypeStruct((BATCH, WINDOW, 128), F32),
                 jax.ShapeDtypeStruct((BATCH, WINDOW, 128), F32),
                 jax.ShapeDtypeStruct((D_MODEL, D_FF), BF16),
                 jax.ShapeDtypeStruct((D_FF, D_MODEL), BF16)]
    out_specs = [pl.BlockSpec((t, D_MODEL), lambda j: (tile_old(j), 0)),
                 sspec(128, 256), sspec(256, 256), sspec(WINDOW, 128), sspec(WINDOW, 128),
                 pl.BlockSpec((up_rows, D_FF), lambda j: (slab(j), 0)),
                 pl.BlockSpec((dn_rows, D_MODEL), lambda j: (slab(j), 0))]
    scratch = [pltpu.VMEM((t, IN_COLS), F32), pltpu.VMEM((t, IN_COLS), F32),
               pltpu.VMEM((t, D_MODEL), BF16), pltpu.VMEM((t, D_MODEL), BF16),
               pltpu.VMEM((128, 256), F32), pltpu.VMEM((256, 256), F32),
               pltpu.VMEM((WINDOW, 128), BF16), pltpu.VMEM((WINDOW, 128), BF16),
               pltpu.VMEM((PROMPT_LEVELS + 1, CHUNK, 256), BF16),
               pltpu.VMEM((PROMPT_LEVELS + 1, CHUNK, 256), BF16),
               pltpu.VMEM((CHUNK, 4 * CHUNK), BF16), pltpu.VMEM((CHUNK, 4 * CHUNK), BF16),
               pltpu.VMEM((CHUNK, 256), BF16), pltpu.VMEM((CHUNK, 256), F32),
               pltpu.VMEM((8, 256), F32)]
    return pl.pallas_call(
        functools.partial(_prompt_mix_kernel, layer=layer),
        grid=(BATCH * seq_tiles + 2,), in_specs=in_specs, out_specs=out_specs,
        out_shape=out_shape, scratch_shapes=scratch, name=f"prompt_mix_{layer}",
        compiler_params=pltpu.CompilerParams(dimension_semantics=("arbitrary",),
                                             vmem_limit_bytes=VMEM_LIMIT),
    )(sinks, x, x, w_in_b, w_out_b, lbraw, hgw, g1, b1, *tabs, *consts, w_up, w_down)


def _ffn_kernel(x_ref, wup_ref, wdn_ref, g_ref, b_ref, *rest):
    if len(rest) == 1:
        (o_ref,) = rest
    else:
        win_f32, wout_f32, o_ref, win_b, wout_b = rest
        win_b[...] = win_f32[...].astype(BF16)
        wout_b[...] = wout_f32[...].astype(BF16)
    x = x_ref[...]
    xb = x.astype(BF16)
    acc = jnp.zeros(x.shape, F32)
    step = 1024
    for i in range(D_FF // step):
        hid = jnp.maximum(jnp.dot(xb, wup_ref[:, i * step:(i + 1) * step],
                                  preferred_element_type=F32), 0.0)
        acc = acc + jnp.dot((hid * hid).astype(BF16), wdn_ref[i * step:(i + 1) * step, :],
                            preferred_element_type=F32)
    o_ref[...] = _layernorm(ALPHA * x + acc, g_ref[...], b_ref[...])


def _ffn(layer, x, w_up_b, w_down_b, g2, b2, next_mixer_weights=None):
    rows = x.shape[0]
    t = min(rows, FFN_TILE)
    steps = rows // t
    in_specs = [pl.BlockSpec((t, D_MODEL), lambda i: (i, 0)),
                pl.BlockSpec((D_MODEL, D_FF), lambda i: (0, 0), pipeline_mode=pl.Buffered(1)),
                pl.BlockSpec((D_FF, D_MODEL), lambda i: (0, 0), pipeline_mode=pl.Buffered(1)),
                pl.BlockSpec((None, 1, D_MODEL), lambda i: (layer, 0, 0)),
                pl.BlockSpec((None, 1, D_MODEL), lambda i: (layer, 0, 0))]
    out_specs = [pl.BlockSpec((t, D_MODEL), lambda i: (i, 0))]
    out_shape = [jax.ShapeDtypeStruct((rows, D_MODEL), F32)]
    args = [x, w_up_b, w_down_b, g2, b2]
    if next_mixer_weights is not None:
        slab = D_MODEL // steps
        for w in next_mixer_weights:
            cols = w.shape[2]
            in_specs.append(pl.BlockSpec((None, slab, cols), lambda i: (layer + 1, i, 0)))
            out_specs.append(pl.BlockSpec((slab, cols), lambda i: (i, 0)))
            out_shape.append(jax.ShapeDtypeStruct((D_MODEL, cols), BF16))
            args.append(w)
    return pl.pallas_call(
        _ffn_kernel, grid=(steps,), in_specs=in_specs, out_specs=out_specs, out_shape=out_shape,
        name=f"ffn_{layer}_{rows}",
        compiler_params=pltpu.CompilerParams(dimension_semantics=("arbitrary",),
                                             vmem_limit_bytes=VMEM_LIMIT),
    )(*args)


def _sample_mix_kernel(sinks_ref, x_ref, win_ref, wout_ref, lbraw_ref, hgw_ref, g1_ref, b1_ref,
                       cr_ref, sr_ref, cs_ref, ss_ref, sret_ref, shg_ref, ck_ref, cv_ref,
                       dec_ref, inter_ref, kdec_ref, mall_ref, hblk_ref,
                       nk_all, nv_all,
                       x1_ref, nsret_ref, nshg_ref, nk_ref, nv_ref, os_scr, *, layer, gam4):
    del nk_all, nv_all
    rows, seqs = SAMPLE_ROWS, SAMPLE_SEQS
    x = x_ref[...]
    h = jnp.dot(x.astype(BF16), win_ref[...], preferred_element_type=F32)
    cr, sr, cs, ss = cr_ref[...], sr_ref[...], cs_ref[...], ss_ref[...]
    lb = _hg_lower_bound(lbraw_ref, layer)

    q = _rope(h[:, C_RQ:C_RQ + 128], cr, sr, RET_QK // 2, RET_QK)
    k = _rope(h[:, C_RK:C_RK + 128], cr, sr, RET_QK // 2, RET_QK) * (RET_QK ** -0.5)
    v = h[:, C_RV:C_RV + 256]
    o_r = _ret_intra(q, k, v, dec_ref, rows)
    hk = RET_HEADS * RET_QK
    s0 = sret_ref[...].reshape(seqs * hk, RET_V)
    r_i = (_iota((seqs * hk, 256), 0) % hk) // RET_QK
    c_i = _iota((seqs * hk, 256), 1) // RET_V
    s0bd = jnp.where(r_i == c_i, jnp.tile(s0, (1, 4)), 0.0).astype(BF16)
    own = _iota((rows, seqs * hk), 0) // DEC_SEQ == _iota((rows, seqs * hk), 1) // hk
    o_r = o_r + _mm(jnp.where(own, jnp.tile(q, (1, seqs)), 0.0), s0bd) * inter_ref[...]
    kdt = (k * kdec_ref[...]).T
    col_seq = _iota((hk, rows), 1) // DEC_SEQ
    for n in range(seqs):
        upd = _mm(jnp.where(col_seq == n, kdt, 0.0), v)
        for hd in range(RET_HEADS):
            nsret_ref[n, hd] = (gam4[hd] * sret_ref[n, hd]
                                + upd[RET_QK * hd:RET_QK * (hd + 1), RET_V * hd:RET_V * (hd + 1)])
    ret = _rms_gate(o_r, h[:, C_RG:C_RG + 256], hblk_ref)

    q = h[:, C_HQ:C_HQ + 256]
    lf = _hg_log_forget(h[:, C_HF:C_HF + 256], lb)
    k = 1.0 - jnp.exp(lf)
    v = h[:, C_HI:C_HI + 256]
    o_h, b, tail = _hgrn_intra(q, k, v, lf, mall_ref, hblk_ref, rows, SAMPLE_LEVELS)
    hk = HG_HEADS * HG_K
    s0 = shg_ref[...].reshape(seqs * hk, HG_V)
    r_i = (_iota((seqs * hk, 256), 0) % hk) // HG_K
    c_i = _iota((seqs * hk, 256), 1) // HG_V
    s0bd = jnp.where(r_i == c_i, jnp.tile(s0, (1, 4)), 0.0).astype(BF16)
    own = _iota((rows, seqs * hk), 0) // DEC_SEQ == _iota((rows, seqs * hk), 1) // hk
    o_h = o_h + _mm(jnp.where(own, jnp.tile(q * jnp.exp(b), (1, seqs)), 0.0), s0bd)
    kdt = (k * jnp.exp(tail)).T
    ebt = jnp.exp(b).T
    col_seq = _iota((hk, rows), 1) // DEC_SEQ
    for n in range(seqs):
        upd = _mm(jnp.where(col_seq == n, kdt, 0.0), v)
        last = DEC_SEQ * n + DEC_SEQ - 1
        dcol = ebt[:, last:last + 1]
        for hd in range(HG_HEADS):
            nshg_ref[n, hd] = (dcol[HG_K * hd:HG_K * (hd + 1)] * shg_ref[n, hd]
                               + upd[HG_K * hd:HG_K * (hd + 1), HG_V * hd:HG_V * (hd + 1)])
    hg = _rms_gate(o_h, h[:, C_HG:C_HG + 256], hblk_ref, hgw_ref[...])

    k = _rope(h[:, C_SK:C_SK + 128], cs, ss, ROT_DIM // 2, SWA_HD)
    v = h[:, C_SV:C_SV + 128]
    kb, vb = k.astype(BF16), v.astype(BF16)
    qm = _swa_heads(h[:, C_SQ:C_SQ + 512], lambda t: _rope(t, cs, ss, ROT_DIM // 2, SWA_HD))
    r64 = _iota((64, 1), 0)
    sinkcol = jnp.zeros((64, 1), F32)
    for hd in range(SWA_HEADS):
        sinkcol = jnp.where(r64 // 8 == hd, sinks_ref[layer, hd], sinkcol)
    rc = _iota((64, 128), 0)
    cc = _iota((64, 128), 1)
    first_seq = (rc % 8) < DEC_SEQ
    cmask = cc >= rc % DEC_SEQ
    rn = _iota((64, rows), 0)
    cn = _iota((64, rows), 1)
    new_lanes = _iota((128, 128), 1) >= WINDOW - DEC_SEQ
    pad_rows = jnp.zeros((WINDOW - 8, 128), F32)
    for pr in range(seqs // 2):
        na, nb = 2 * pr, 2 * pr + 1
        qp = jnp.concatenate([t[8 * pr:8 * pr + 8] for t in qm], axis=0)
        ka, kb_c = ck_ref[na], ck_ref[nb]
        va, vb_c = cv_ref[na], cv_ref[nb]
        sc = jnp.where(first_seq, _mm(qp, ka), _mm(qp, kb_c))
        sc = jnp.where(cmask, sc, NEG)
        nmask = (cn // DEC_SEQ == 2 * pr + (rn % 8) // DEC_SEQ) & (cn % DEC_SEQ <= rn % DEC_SEQ)
        sn = jnp.where(nmask, _mm_nt(qp, kb), NEG)
        m = jnp.maximum(jnp.maximum(jnp.max(sc, axis=1, keepdims=True),
                                    jnp.max(sn, axis=1, keepdims=True)), sinkcol)
        pc = jnp.exp(sc - m)
        pn = jnp.exp(sn - m)
        den = (jnp.sum(pc, axis=1, keepdims=True) + jnp.sum(pn, axis=1, keepdims=True)
               + jnp.exp(sinkcol - m))
        o = (_mm_nt(jnp.where(first_seq, pc, 0.0), va) + _mm_nt(jnp.where(first_seq, 0.0, pc), vb_c)
             + _mm(pn, vb)) / den
        pieces = []
        for hd in range(SWA_HEADS):
            g = hd // SWA_GROUP
            pieces.append(o[8 * hd:8 * hd + 8, SWA_HD * g:SWA_HD * (g + 1)])
        os_scr[8 * pr:8 * pr + 8, :] = jnp.concatenate(pieces, axis=1)
        for new, old_a, old_b, dst in ((k[8 * pr:8 * pr + 8], ka, kb_c, nk_ref),
                                       (v[8 * pr:8 * pr + 8], va, vb_c, nv_ref)):
            new_t = jnp.concatenate([new, pad_rows], axis=0).T
            dst[na] = jnp.where(new_lanes, pltpu.roll(new_t, WINDOW - DEC_SEQ, 1),
                                pltpu.roll(old_a, WINDOW - DEC_SEQ, 1))
            dst[nb] = jnp.where(new_lanes, pltpu.roll(new_t, WINDOW - 2 * DEC_SEQ, 1),
                                pltpu.roll(old_b, WINDOW - DEC_SEQ, 1))

    cat = jnp.concatenate([ret, hg, os_scr[...]], axis=1).astype(BF16)
    mix = jnp.dot(cat, wout_ref[...], preferred_element_type=F32)
    x1_ref[...] = _layernorm(ALPHA * x + mix, g1_ref[...], b1_ref[...])


def _sample_mix(layer, x, w_in, w_out, sinks, lbraw, hgw, g1, b1, tabs, state_ret, state_hgrn,
                cache_k, cache_v, consts, gam4, stacked):
    r, s = SAMPLE_ROWS, SAMPLE_SEQS
    wspec = lambda a, c: pl.BlockSpec((a, c), lambda i: (0, 0), pipeline_mode=pl.Buffered(1))
    vspec = lambda c: pl.BlockSpec((None, 1, c), lambda i: (layer, 0, 0))
    in_specs = [pl.BlockSpec(memory_space=pltpu.SMEM),
                pl.BlockSpec((r, D_MODEL), lambda i: (i, 0)),
                wspec(D_MODEL, IN_COLS), wspec(D_MODEL, D_MODEL),
                _const_spec((DEPTH, 256), 1), vspec(256), vspec(D_MODEL), vspec(D_MODEL)]
    in_specs += [_const_spec((r, LANES), 1)] * 4
    first_state = len(in_specs)
    in_specs += [pl.BlockSpec((None, s, RET_HEADS, RET_QK, RET_V), lambda i: (layer, i, 0, 0, 0)),
                 pl.BlockSpec((None, s, HG_HEADS, HG_K, HG_V), lambda i: (layer, i, 0, 0, 0)),
                 pl.BlockSpec((None, s, WINDOW, 128), lambda i: (layer, i, 0, 0)),
                 pl.BlockSpec((None, s, WINDOW, 128), lambda i: (layer, i, 0, 0))]
    in_specs += [_const_spec(c.shape, 1) for c in consts]
    first_stacked = len(in_specs)
    in_specs += [pl.BlockSpec(memory_space=pl.ANY)] * len(stacked)
    out_shape = [jax.ShapeDtypeStruct((DEC_BATCH * DEC_SEQ, D_MODEL), F32)]
    out_shape += [jax.ShapeDtypeStruct(a.shape, a.dtype)
                  for a in (state_ret, state_hgrn) + tuple(stacked)]
    out_specs = [pl.BlockSpec((r, D_MODEL), lambda i: (i, 0)),
                 pl.BlockSpec((None, s, RET_HEADS, RET_QK, RET_V), lambda i: (layer, i, 0, 0, 0)),
                 pl.BlockSpec((None, s, HG_HEADS, HG_K, HG_V), lambda i: (layer, i, 0, 0, 0)),
                 pl.BlockSpec((None, s, WINDOW, 128), lambda i: (layer, i, 0, 0)),
                 pl.BlockSpec((None, s, WINDOW, 128), lambda i: (layer, i, 0, 0))]
    return pl.pallas_call(
        functools.partial(_sample_mix_kernel, layer=layer, gam4=gam4),
        grid=(DEC_BATCH // s,), in_specs=in_specs, out_specs=out_specs, out_shape=out_shape,
        input_output_aliases={first_state: 1, first_state + 1: 2,
                              first_stacked: 3, first_stacked + 1: 4},
        scratch_shapes=[pltpu.VMEM((r, 512), F32)], name=f"sample_mix_{layer}",
        compiler_params=pltpu.CompilerParams(dimension_semantics=("arbitrary",),
                                             vmem_limit_bytes=VMEM_LIMIT),
    )(sinks, x, w_in, w_out, lbraw, hgw, g1, b1, *tabs, state_ret, state_hgrn, cache_k, cache_v,
      *consts, *stacked)


def _decay_levels(rows, seqlen, levels):
    t = np.arange(rows)[:, None]
    u = np.arange(rows)[None, :]
    same = (t // seqlen) == (u // seqlen)
    blocks = [same & (u <= t)]
    for l in range(levels):
        m = 1 << l
        r = (t // (2 * m)) * (2 * m) + m
        lower = t >= r
        blocks.append(np.where(lower, (u >= r) & (u <= t), (u > t) & (u <= r - 1)))
    blocks.append(same & (u > t))
    return jnp.asarray(np.concatenate(blocks, axis=0).astype(np.float32), dtype=BF16)


def _rope_tables(pos):
    pos = pos.astype(F32)[:, None]
    inv = RET_THETA ** (-jnp.linspace(0.0, 1.0, RET_QK // 2, dtype=F32))
    ang = pos * inv[None, :]
    cr = jnp.tile(jnp.concatenate([jnp.cos(ang), jnp.cos(ang)], -1), (1, RET_HEADS))
    sr = jnp.tile(jnp.concatenate([-jnp.sin(ang), jnp.sin(ang)], -1), (1, RET_HEADS))
    inv = ROPE_THETA ** (-jnp.arange(0, ROT_DIM, 2, dtype=F32) / ROT_DIM)
    ang = pos * inv[None, :]
    rest = SWA_HD - ROT_DIM
    ones = jnp.ones((ang.shape[0], rest), F32)
    zeros = jnp.zeros((ang.shape[0], rest), F32)
    cs = jnp.tile(jnp.concatenate([jnp.cos(ang), jnp.cos(ang), ones], -1), (1, 2))
    ss = jnp.tile(jnp.concatenate([-jnp.sin(ang), jnp.sin(ang), zeros], -1), (1, 2))
    return cr, sr, cs, ss


def _ret_tables(tok, same):
    lg = jnp.log(1.0 - 2.0 ** (-5.0 - jnp.arange(RET_HEADS, dtype=F32)))
    n_tok = int(tok.max()) + 1
    tokf = jnp.asarray(tok, F32)
    diff = tokf[:, None] - tokf[None, :]
    dec = jnp.where(jnp.asarray(same) & (diff >= 0),
                    jnp.exp(jnp.maximum(diff, 0.0)[None] * lg[:, None, None]), 0.0)
    inter = jnp.repeat(jnp.exp((tokf + 1.0)[:, None] * lg[None, :]), RET_V, axis=1)
    kdec = jnp.repeat(jnp.exp((n_tok - 1.0 - tokf)[:, None] * lg[None, :]), RET_QK, axis=1)
    total = jnp.exp(n_tok * lg)
    return dec, inter, kdec, total


def kernel(x_prompt, x_sample, state_ret, state_hgrn, cache_swa_k, cache_swa_v, w_in, w_out,
           swa_sinks, hg_lb_raw, hg_norm_w, ln_mix_g, ln_mix_b, w_up, w_down, ln_ffn_g, ln_ffn_b):
    w_in_b, w_out_b = w_in[0].astype(BF16), w_out[0].astype(BF16)
    hgw = jnp.tile(hg_norm_w, (1, HG_HEADS)).reshape(DEPTH, 1, HG_HEADS * HG_V)
    g1, b1 = ln_mix_g.reshape(DEPTH, 1, D_MODEL), ln_mix_b.reshape(DEPTH, 1, D_MODEL)
    g2, b2 = ln_ffn_g.reshape(DEPTH, 1, D_MODEL), ln_ffn_b.reshape(DEPTH, 1, D_MODEL)
    cache_k = cache_swa_k.transpose(0, 1, 3, 4, 2).reshape(DEPTH, DEC_BATCH, 128, WINDOW)
    cache_v = cache_swa_v.transpose(0, 1, 3, 4, 2).reshape(DEPTH, DEC_BATCH, 128, WINDOW)

    hblk = jnp.asarray((np.arange(256)[:, None] // 64 == np.arange(256)[None, :] // 64)
                       .astype(np.float32), dtype=BF16)
    p_tabs = _rope_tables(jnp.arange(SEQ))
    tok = np.arange(CHUNK)
    dec, inter, kdec, total = _ret_tables(tok, np.ones((CHUNK, CHUNK), bool))
    rowdec = jnp.broadcast_to(jnp.repeat(total, RET_QK)[:, None], (128, 256))
    t_i, s_i = np.arange(CHUNK)[:, None], np.arange(CHUNK)[None, :]
    tri = jnp.asarray((s_i <= t_i).astype(np.float32), dtype=BF16)
    lvl = np.where(s_i < t_i, np.floor(np.log2(np.maximum(t_i ^ s_i, 1))),
                   np.where(s_i == t_i, PROMPT_LEVELS, -1)).astype(np.int32)
    q_i, k_j = np.arange(CHUNK)[:, None], np.arange(2 * CHUNK)[None, :]
    band = np.where((k_j >= q_i) & (k_j <= q_i + WINDOW), np.where(k_j >= WINDOW, 2, 1), 0)
    p_consts = (dec, inter, kdec, rowdec, tri, hblk, jnp.asarray(lvl),
                jnp.asarray(band.astype(np.int32)))
    row = np.arange(SAMPLE_ROWS)
    s_tabs = _rope_tables(PAST_LEN + jnp.asarray(row % DEC_SEQ))
    same = (row[:, None] // DEC_SEQ) == (row[None, :] // DEC_SEQ)
    dec_s, inter_s, kdec_s, total_s = _ret_tables(row % DEC_SEQ, same)
    s_consts = (dec_s, inter_s, kdec_s, _decay_levels(SAMPLE_ROWS, DEC_SEQ, SAMPLE_LEVELS), hblk)
    gam4 = tuple(float((1.0 - 2.0 ** (-5.0 - h)) ** DEC_SEQ) for h in range(RET_HEADS))
    del total_s

    xp = x_prompt.reshape(BATCH * SEQ, D_MODEL)
    xs = x_sample.reshape(DEC_BATCH * DEC_SEQ, D_MODEL)
    outs = [[] for _ in range(4)]
    ns_r, ns_h = state_ret, state_hgrn
    stacked = (jnp.zeros((DEPTH, DEC_BATCH, 128, WINDOW), F32),
               jnp.zeros((DEPTH, DEC_BATCH, 128, WINDOW), F32))
    for l in range(DEPTH):
        x1, st_r, st_h, pk, pv, w_up_b, w_down_b = _prompt_mix(
            l, xp, w_in_b, w_out_b, swa_sinks, hg_lb_raw, hgw, g1, b1, p_tabs, p_consts,
            w_up, w_down)
        ffn_out = _ffn(l, x1, w_up_b, w_down_b, g2, b2,
                       (w_in, w_out) if l + 1 < DEPTH else None)
        xp = ffn_out[0]
        outs[0].append(jnp.stack([st_r[:, RET_QK * h:RET_QK * (h + 1), RET_V * h:RET_V * (h + 1)]
                                  for h in range(RET_HEADS)], axis=1))
        outs[1].append(jnp.stack([st_h[:, HG_K * h:HG_K * (h + 1), HG_V * h:HG_V * (h + 1)]
                                  for h in range(HG_HEADS)], axis=1))
        outs[2].append(pk.reshape(BATCH, SWA_KV, SWA_HD, WINDOW).transpose(0, 3, 1, 2))
        outs[3].append(pv.reshape(BATCH, SWA_KV, SWA_HD, WINDOW).transpose(0, 3, 1, 2))

        x1, ns_r, ns_h, *stacked = _sample_mix(l, xs, w_in_b, w_out_b, swa_sinks, hg_lb_raw, hgw,
                                               g1, b1, s_tabs, ns_r, ns_h, cache_k, cache_v,
                                               s_consts, gam4, tuple(stacked))
        xs = _ffn(l, x1, w_up_b, w_down_b, g2, b2)[0]
        if l + 1 < DEPTH:
            w_in_b, w_out_b = ffn_out[1], ffn_out[2]

    nk, nv = stacked
    nk = nk.reshape(DEPTH, DEC_BATCH, SWA_KV, SWA_HD, WINDOW).transpose(0, 1, 4, 2, 3)
    nv = nv.reshape(DEPTH, DEC_BATCH, SWA_KV, SWA_HD, WINDOW).transpose(0, 1, 4, 2, 3)
    return ((xp.reshape(BATCH, SEQ, D_MODEL), xs.reshape(DEC_BATCH, DEC_SEQ, D_MODEL))
            + tuple(jnp.stack(o) for o in outs) + (ns_r, ns_h, nk, nv))
```

```python
import functools

import numpy as np
import jax
import jax.numpy as jnp
from jax import lax
from jax.experimental import pallas as pl
from jax.experimental.pallas import tpu as pltpu

F32 = jnp.float32
BF16 = jnp.bfloat16

D_MODEL = 1024
BATCH = 2
SEQ = 8192
DEPTH = 4
DEC_BATCH = 128
DEC_SEQ = 4
PAST_LEN = 8192
RET_HEADS = 4
RET_QK = 32
RET_V = 64
RET_THETA = 10000.0
HG_HEADS = 4
HG_K = 64
HG_V = 64
SWA_HEADS = 8
SWA_KV = 2
SWA_GROUP = SWA_HEADS // SWA_KV
SWA_HD = 64
WINDOW = 128
ROT_DIM = 16
ROPE_THETA = 500000.0
CHUNK = 128
D_FF = 4 * D_MODEL
IN_COLS = 2560
ALPHA = (2.0 * DEPTH) ** 0.25
LN_EPS = 1e-5
RMS_EPS = 1e-6
NEG = -1e30

C_RQ, C_RK, C_RV, C_RG = 0, 128, 256, 512
C_HQ, C_HF, C_HI, C_HG = 768, 1024, 1280, 1536
C_SQ, C_SK, C_SV = 1792, 2304, 2432

LANES = 128
PROMPT_TILE = 512
FFN_TILE = 512
SAMPLE_SEQS = 16
SAMPLE_ROWS = SAMPLE_SEQS * DEC_SEQ
PROMPT_LEVELS = 7
SAMPLE_LEVELS = 2
VMEM_LIMIT = 56 * 1024 * 1024


def _mm(a, b):
    return jnp.dot(a.astype(BF16), b.astype(BF16), preferred_element_type=F32)


def _mm_nt(a, b):
    return lax.dot_general(a.astype(BF16), b.astype(BF16), (((1,), (1,)), ((), ())),
                           preferred_element_type=F32)


def _iota(shape, dim):
    return lax.broadcasted_iota(jnp.int32, shape, dim)


def _layernorm(x, g, b):
    mu = jnp.mean(x, axis=-1, keepdims=True)
    xc = x - mu
    var = jnp.mean(xc * xc, axis=-1, keepdims=True)
    return xc * lax.rsqrt(var + LN_EPS) * g + b


def _rope(x, cos_t, sin_t, half, period):
    first_half = (_iota((1, LANES), 1) % period) < half
    fwd = pltpu.roll(x, LANES - half, 1)
    bwd = pltpu.roll(x, half, 1)
    return x * cos_t + jnp.where(first_half, fwd, bwd) * sin_t


def _silu(x):
    return x * (1.0 / (1.0 + jnp.exp(-x)))


def _split3(x):
    hi = x.astype(BF16)
    r = x - hi.astype(F32)
    mid = r.astype(BF16)
    lo = (r - mid.astype(F32)).astype(BF16)
    return hi, mid, lo


def _hg_lower_bound(lbraw_ref, layer):
    raw = lbraw_ref[...]
    e = jnp.exp(raw - jnp.max(raw, axis=0, keepdims=True))
    sm = e / jnp.sum(e, axis=0, keepdims=True)
    cum = sm[0:1]
    for i in range(1, layer + 1):
        cum = cum + sm[i:i + 1]
    return cum - sm[0:1]


def _hg_log_forget(hf, lb):
    logsig = jnp.minimum(hf, 0.0) - jnp.log(1.0 + jnp.exp(-jnp.abs(hf)))
    a = jnp.log(lb)
    b = jnp.log(1.0 - lb) + logsig
    return jnp.maximum(a, b) + jnp.log(1.0 + jnp.exp(-jnp.abs(a - b)))


def _level_exponents(b, rows, levels):
    g = rows // 8
    n = b.shape[1]
    b3 = b.reshape(g, 8, n)
    sub = _iota((g, 8, n), 1)
    dn1 = pltpu.roll(b3, 1, 1)
    dn2 = pltpu.roll(b3, 2, 1)
    up1 = pltpu.roll(b3, 7, 1)
    out = [jnp.where(sub % 2 == 1, b3 - dn1, 0.0)]
    m4 = sub % 4
    out.append(jnp.where(m4 == 0, up1 - b3,
                         jnp.where(m4 == 1, 0.0, jnp.where(m4 == 2, b3 - dn1, b3 - dn2))))
    if levels > 2:
        r3 = jnp.broadcast_to(b3[:, 3:4, :], b3.shape)
        out.append(jnp.where(sub < 4, r3 - b3, b3 - r3))
        r7 = jnp.broadcast_to(b3[:, 7:8, :], b3.shape)
        grp = _iota((g, 8, n), 0)
        for l in range(3, levels):
            span = (2 << l) // 8
            half = span // 2
            ref = r7.reshape(g // span, span, 8, n)[:, half - 1:half]
            ref = jnp.broadcast_to(ref, (g // span, span, 8, n)).reshape(g, 8, n)
            out.append(jnp.where(grp % span >= half, b3 - ref, ref - b3))
    return [e.reshape(rows, n) for e in out]


def _head_blockdiag(v, rows):
    r = _iota((4 * rows, 256), 0) // rows
    c = _iota((4 * rows, 256), 1) // 64
    return jnp.where(r == c, jnp.tile(v, (4, 1)), 0.0).astype(BF16)


def _rms_gate(o, gate, hblk_ref, w=None):
    ms = _mm(o * o, hblk_ref[...]) * (1.0 / 64.0)
    y = o * lax.rsqrt(ms + RMS_EPS)
    if w is not None:
        y = y * w
    return y * _silu(gate)


def _ret_intra(q, k, v, dec_ref, rows):
    lane_head = _iota((rows, 128), 1) // RET_QK
    a = [_mm_nt(jnp.where(lane_head == h, q, 0.0), k) * dec_ref[h] for h in range(RET_HEADS)]
    return _mm(jnp.concatenate(a, axis=1), _head_blockdiag(v, rows))


def _hgrn_intra(q, k, v, lf, mall_ref, hblk_ref, rows, levels):
    l3 = jnp.concatenate(_split3(lf), axis=1)
    eb = jnp.dot(mall_ref[...], l3, preferred_element_type=F32)
    eb = eb[:, 0:256] + eb[:, 256:512] + eb[:, 512:768]
    b = eb[0:rows]
    tail = eb[(levels + 1) * rows:(levels + 2) * rows]
    row = _iota((rows, 256), 0)
    lane_head = _iota((rows, 256), 1) // HG_K
    ra = _iota((rows, rows), 0)
    ca = _iota((rows, rows), 1)
    acc = [jnp.zeros((rows, rows), F32) for _ in range(HG_HEADS)]
    for l in range(levels):
        w = jnp.exp(eb[(l + 1) * rows:(l + 2) * rows])
        lower = ((row >> l) & 1) == 1
        qs = jnp.where(lower, q * w, 0.0).astype(BF16)
        ks = jnp.where(lower, 0.0, k * w).astype(BF16)
        same_block = (ra >> (l + 1)) == (ca >> (l + 1))
        for h in range(HG_HEADS):
            p = _mm_nt(jnp.where(lane_head == h, qs, jnp.zeros_like(qs)), ks)
            acc[h] = acc[h] + jnp.where(same_block, p, 0.0)
    o = _mm(jnp.concatenate(acc, axis=1), _head_blockdiag(v, rows))
    o = o + _mm(q * k, hblk_ref[...]) * v
    return o, b, tail


def _swa_heads(qcols, rope_q, scale=SWA_HD ** -0.5):
    out = []
    for c in range(4):
        qc = rope_q(qcols[:, LANES * c:LANES * (c + 1)])
        if scale is not None:
            qc = qc * scale
        half = _iota(qc.shape, 1) // SWA_HD
        for p in range(2):
            g = (2 * c + p) // SWA_GROUP
            qm = jnp.where(half == p, qc, 0.0)
            if p != g:
                qm = pltpu.roll(qm, SWA_HD, 1)
            out.append(qm)
    return out


def _prompt_mix_kernel(sinks_ref, x_ref, xold_ref, win_ref, wout_ref, lbraw_ref, hgw_ref, g1_ref,
                       b1_ref, cr_ref, sr_ref, cs_ref, ss_ref, dec_ref, inter_ref, kdec_ref,
                       rowdec_ref, tri_ref, hblk_ref, lvl_ref, band_ref, wup_f32, wdn_f32,
                       x1_ref, str_ref, sth_ref, pk_ref, pv_ref, wup_b, wdn_b,
                       h_new, h_scr, cat_scr, cat_old, sret, shg, kprev, vprev, qs_scr, ks_scr,
                       ar_scr, ah_scr, qb_scr, kb_scr, bl_scr, *, layer):
    j = pl.program_id(0)
    seq_tiles = SEQ // PROMPT_TILE
    n_tiles = BATCH * seq_tiles
    n_chunks = PROMPT_TILE // CHUNK
    levels = PROMPT_LEVELS
    mix_tile = (j - 1) % seq_tiles

    wup_b[...] = wup_f32[...].astype(BF16)
    wdn_b[...] = wdn_f32[...].astype(BF16)

    @pl.when(mix_tile == 0)
    def _():
        sret[...] = jnp.zeros_like(sret)
        shg[...] = jnp.zeros_like(shg)
        kprev[...] = jnp.zeros_like(kprev)
        vprev[...] = jnp.zeros_like(vprev)

    n_parts = 5
    part_cols = IN_COLS // n_parts

    def project(rows, part):
        cols = slice(part * part_cols, (part + 1) * part_cols)
        h_new[rows, cols] = jnp.dot(x_ref[rows, :].astype(BF16), win_ref[:, cols],
                                    preferred_element_type=F32)

    def finish(rows):
        mix = jnp.dot(cat_scr[rows, :], wout_ref[...], preferred_element_type=F32)
        x1_ref[rows, :] = _layernorm(ALPHA * xold_ref[rows, :] + mix, g1_ref[...], b1_ref[...])

    def mixers(c, rows, filler):
        lb = _hg_lower_bound(lbraw_ref, layer)
        cr, sr = cr_ref[rows, :], sr_ref[rows, :]
        cs, ss = cs_ref[rows, :], ss_ref[rows, :]

        q = _rope(h_scr[rows, C_RQ:C_RQ + 128], cr, sr, RET_QK // 2, RET_QK)
        k = _rope(h_scr[rows, C_RK:C_RK + 128], cr, sr, RET_QK // 2, RET_QK) * (RET_QK ** -0.5)
        v = h_scr[rows, C_RV:C_RV + 256]
        lane_head = _iota((CHUNK, 128), 1) // RET_QK
        for hd in range(RET_HEADS):
            a = _mm_nt(jnp.where(lane_head == hd, q, 0.0), k) * dec_ref[hd]
            ar_scr[:, CHUNK * hd:CHUNK * (hd + 1)] = a.astype(BF16)
        o_r = jnp.dot(ar_scr[...], _head_blockdiag(v, CHUNK), preferred_element_type=F32)
        s0 = sret[...]
        o_r = o_r + _mm(q, s0) * inter_ref[...]
        r_i = _iota((128, 256), 0) // RET_QK
        c_i = _iota((128, 256), 1) // RET_V
        upd = _mm((k * kdec_ref[...]).T, v)
        sret[...] = rowdec_ref[...] * s0 + jnp.where(r_i == c_i, upd, 0.0)
        cat_scr[rows, 0:256] = _rms_gate(o_r, h_scr[rows, C_RG:C_RG + 256], hblk_ref).astype(BF16)
        filler(0)

        q = h_scr[rows, C_HQ:C_HQ + 256]
        lf = _hg_log_forget(h_scr[rows, C_HF:C_HF + 256], lb)
        k = 1.0 - jnp.exp(lf)
        l3 = jnp.concatenate(_split3(lf), axis=1)
        b = jnp.dot(tri_ref[...], l3, preferred_element_type=F32)
        b = b[:, 0:256] + b[:, 256:512] + b[:, 512:768]
        for l, e in enumerate(_level_exponents(b, CHUNK, levels)):
            w = jnp.exp(e)
            qs_scr[l] = (q * w).astype(BF16)
            ks_scr[l] = (k * w).astype(BF16)
        qs_scr[levels] = q.astype(BF16)
        ks_scr[levels] = k.astype(BF16)
        bl = b[CHUNK - 1:CHUNK, :]
        qb_scr[...] = (q * jnp.exp(b)).astype(BF16)
        kb_scr[...] = k * jnp.exp(bl - b)
        bl_scr[...] = jnp.broadcast_to(jnp.exp(bl), (8, 256))
        v = h_scr[rows, C_HI:C_HI + 256]
        filler(1)
        lvl = lvl_ref[...]
        for hd in range(HG_HEADS):
            lanes = slice(HG_K * hd, HG_K * (hd + 1))
            a = jnp.zeros((CHUNK, CHUNK), F32)
            for l in range(levels + 1):
                p = lax.dot_general(qs_scr[l, :, lanes], ks_scr[l, :, lanes],
                                    (((1,), (1,)), ((), ())), preferred_element_type=F32)
                a = jnp.where(lvl == l, p, a)
            ah_scr[:, CHUNK * hd:CHUNK * (hd + 1)] = a.astype(BF16)
        o_h = jnp.dot(ah_scr[...], _head_blockdiag(v, CHUNK), preferred_element_type=F32)
        s0 = shg[...]
        o_h = o_h + jnp.dot(qb_scr[...], s0.astype(BF16), preferred_element_type=F32)
        decay_t = jnp.broadcast_to(bl_scr[0:1, :], (128, 256)).T
        r_i = _iota((256, 256), 0) // HG_K
        c_i = _iota((256, 256), 1) // HG_V
        upd = _mm(kb_scr[...].T, v)
        shg[...] = jnp.tile(decay_t, (1, 2)) * s0 + jnp.where(r_i == c_i, upd, 0.0)
        cat_scr[rows, 256:512] = _rms_gate(o_h, h_scr[rows, C_HG:C_HG + 256], hblk_ref,
                                           hgw_ref[...]).astype(BF16)
        filler(2)

        k = _rope(h_scr[rows, C_SK:C_SK + 128], cs, ss, ROT_DIM // 2, SWA_HD)
        v = h_scr[rows, C_SV:C_SV + 128]
        ks_b = (k * (SWA_HD ** -0.5)).astype(BF16)
        kk = jnp.concatenate([kprev[...], ks_b], axis=0)
        vv = jnp.concatenate([vprev[...], v.astype(BF16)], axis=0)
        mask = band_ref[...] > jnp.where(mix_tile * n_chunks + c > 0, 0, 1)
        qm = _swa_heads(h_scr[rows, C_SQ:C_SQ + 512],
                        lambda t: _rope(t, cs, ss, ROT_DIM // 2, SWA_HD), scale=None)
        outs = []
        for hd in range(SWA_HEADS):
            g = hd // SWA_GROUP
            sink = sinks_ref[layer, hd]
            s = jnp.where(mask, _mm_nt(qm[hd], kk), NEG)
            m = jnp.maximum(jnp.max(s, axis=1, keepdims=True), sink)
            p = jnp.exp(s - m)
            den = jnp.sum(p, axis=1, keepdims=True) + jnp.exp(sink - m)
            outs.append(_mm(p, vv)[:, SWA_HD * g:SWA_HD * (g + 1)] / den)
            if hd % 2 == 1:
                col = 512 + LANES * (hd // 2)
                cat_scr[rows, col:col + LANES] = jnp.concatenate(outs, axis=1).astype(BF16)
                outs = []
            if hd == SWA_HEADS // 2 - 1:
                filler(3)
        filler(4)
        kprev[...] = ks_b
        vprev[...] = v.astype(BF16)
        pk_ref[...] = k.T
        pv_ref[...] = v.T


    def run(do_project, do_mixers, do_finish):
        def body(c, carry):
            rows = pl.ds(pl.multiple_of(c * CHUNK, CHUNK), CHUNK)
            fill = (lambda part: project(rows, part)) if do_project else (lambda part: None)
            if do_mixers:
                mixers(c, rows, fill)
            elif do_project:
                for part in range(n_parts):
                    project(rows, part)
            if do_finish:
                finish(rows)
            if do_mixers:
                pass
            if do_project:
                h_scr[rows, :] = h_new[rows, :]
            return carry
        lax.fori_loop(0, n_chunks, body, 0)
        if do_mixers:
            str_ref[...] = sret[...]
            sth_ref[...] = shg[...]

    pl.when(j == 0)(lambda: run(True, False, False))
    pl.when((j >= 1) & (j < n_tiles))(lambda: run(True, True, True))
    pl.when(j == n_tiles)(lambda: run(False, True, True))


def _const_spec(shape, ngrid):
    zeros = (0,) * len(shape)
    if ngrid == 1:
        return pl.BlockSpec(shape, lambda i: zeros)
    return pl.BlockSpec(shape, lambda b, j: zeros)


def _prompt_mix(layer, x, w_in_b, w_out_b, sinks, lbraw, hgw, g1, b1, tabs, consts, w_up, w_down):
    t = PROMPT_TILE
    seq_tiles = SEQ // t
    wspec = lambda r, c: pl.BlockSpec((r, c), lambda j: (0, 0), pipeline_mode=pl.Buffered(1))
    vspec = lambda c: pl.BlockSpec((None, 1, c), lambda j: (layer, 0, 0))
    cast_slabs = BATCH * seq_tiles
    slab = lambda j: jnp.minimum(j, cast_slabs - 1)
    last = BATCH * seq_tiles - 1
    tile_new = lambda j: jnp.minimum(j, last)
    tile_mix = lambda j: jnp.clip(j - 1, 0, last)
    tile_old = tile_mix
    tspec = pl.BlockSpec((t, LANES), lambda j: (tile_mix(j) % seq_tiles, 0))
    sspec = lambda r, c: pl.BlockSpec((None, r, c), lambda j: (tile_mix(j) // seq_tiles, 0, 0))
    in_specs = [pl.BlockSpec(memory_space=pltpu.SMEM),
                pl.BlockSpec((t, D_MODEL), lambda j: (tile_new(j), 0)),
                pl.BlockSpec((t, D_MODEL), lambda j: (tile_old(j), 0)),
                wspec(D_MODEL, IN_COLS), wspec(D_MODEL, D_MODEL),
                _const_spec((DEPTH, 256), 1), vspec(256), vspec(D_MODEL), vspec(D_MODEL),
                tspec, tspec, tspec, tspec] + [_const_spec(c.shape, 1) for c in consts]
    up_rows, dn_rows = D_MODEL // cast_slabs, D_FF // cast_slabs
    in_specs += [pl.BlockSpec((None, up_rows, D_FF), lambda j: (layer, slab(j), 0)),
                 pl.BlockSpec((None, dn_rows, D_MODEL), lambda j: (layer, slab(j), 0))]
    out_shape = [jax.ShapeDtypeStruct((BATCH * SEQ, D_MODEL), F32),
                 jax.ShapeDtypeStruct((BATCH, 128, 256), F32),
                 jax.ShapeDtypeStruct((BATCH, 256, 256), F32),
                 jax.ShapeDtypeStruct((BATCH, WINDOW, 128), F32),
                 jax.ShapeDtypeStruct((BATCH, WINDOW, 128), F32),
                 jax.ShapeDtypeStruct((D_MODEL, D_FF), BF16),
                 jax.ShapeDtypeStruct((D_FF, D_MODEL), BF16)]
    out_specs = [pl.BlockSpec((t, D_MODEL), lambda j: (tile_old(j), 0)),
                 sspec(128, 256), sspec(256, 256), sspec(WINDOW, 128), sspec(WINDOW, 128),
                 pl.BlockSpec((up_rows, D_FF), lambda j: (slab(j), 0)),
                 pl.BlockSpec((dn_rows, D_MODEL), lambda j: (slab(j), 0))]
    scratch = [pltpu.VMEM((t, IN_COLS), F32), pltpu.VMEM((t, IN_COLS), F32),
               pltpu.VMEM((t, D_MODEL), BF16), pltpu.VMEM((t, D_MODEL), BF16),
               pltpu.VMEM((128, 256), F32), pltpu.VMEM((256, 256), F32),
               pltpu.VMEM((WINDOW, 128), BF16), pltpu.VMEM((WINDOW, 128), BF16),
               pltpu.VMEM((PROMPT_LEVELS + 1, CHUNK, 256), BF16),
               pltpu.VMEM((PROMPT_LEVELS + 1, CHUNK, 256), BF16),
               pltpu.VMEM((CHUNK, 4 * CHUNK), BF16), pltpu.VMEM((CHUNK, 4 * CHUNK), BF16),
               pltpu.VMEM((CHUNK, 256), BF16), pltpu.VMEM((CHUNK, 256), F32),
               pltpu.VMEM((8, 256), F32)]
    return pl.pallas_call(
        functools.partial(_prompt_mix_kernel, layer=layer),
        grid=(BATCH * seq_tiles + 1,), in_specs=in_specs, out_specs=out_specs,
        out_shape=out_shape, scratch_shapes=scratch, name=f"prompt_mix_{layer}",
        compiler_params=pltpu.CompilerParams(dimension_semantics=("arbitrary",),
                                             vmem_limit_bytes=VMEM_LIMIT),
    )(sinks, x, x, w_in_b, w_out_b, lbraw, hgw, g1, b1, *tabs, *consts, w_up, w_down)


def _ffn_kernel(x_ref, wup_ref, wdn_ref, g_ref, b_ref, *rest):
    if len(rest) == 1:
        (o_ref,) = rest
    else:
        win_f32, wout_f32, o_ref, win_b, wout_b = rest
        win_b[...] = win_f32[...].astype(BF16)
        wout_b[...] = wout_f32[...].astype(BF16)
    x = x_ref[...]
    xb = x.astype(BF16)
    acc = jnp.zeros(x.shape, F32)
    step = 1024
    for i in range(D_FF // step):
        hid = jnp.maximum(jnp.dot(xb, wup_ref[:, i * step:(i + 1) * step],
                                  preferred_element_type=F32), 0.0)
        acc = acc + jnp.dot((hid * hid).astype(BF16), wdn_ref[i * step:(i + 1) * step, :],
                            preferred_element_type=F32)
    o_ref[...] = _layernorm(ALPHA * x + acc, g_ref[...], b_ref[...])


def _ffn(layer, x, w_up_b, w_down_b, g2, b2, next_mixer_weights=None):
    rows = x.shape[0]
    t = min(rows, FFN_TILE)
    steps = rows // t
    in_specs = [pl.BlockSpec((t, D_MODEL), lambda i: (i, 0)),
                pl.BlockSpec((D_MODEL, D_FF), lambda i: (0, 0), pipeline_mode=pl.Buffered(1)),
                pl.BlockSpec((D_FF, D_MODEL), lambda i: (0, 0), pipeline_mode=pl.Buffered(1)),
                pl.BlockSpec((None, 1, D_MODEL), lambda i: (layer, 0, 0)),
                pl.BlockSpec((None, 1, D_MODEL), lambda i: (layer, 0, 0))]
    out_specs = [pl.BlockSpec((t, D_MODEL), lambda i: (i, 0))]
    out_shape = [jax.ShapeDtypeStruct((rows, D_MODEL), F32)]
    args = [x, w_up_b, w_down_b, g2, b2]
    if next_mixer_weights is not None:
        slab = D_MODEL // steps
        for w in next_mixer_weights:
            cols = w.shape[2]
            in_specs.append(pl.BlockSpec((None, slab, cols), lambda i: (layer + 1, i, 0)))
            out_specs.append(pl.BlockSpec((slab, cols), lambda i: (i, 0)))
            out_shape.append(jax.ShapeDtypeStruct((D_MODEL, cols), BF16))
            args.append(w)
    return pl.pallas_call(
        _ffn_kernel, grid=(steps,), in_specs=in_specs, out_specs=out_specs, out_shape=out_shape,
        name=f"ffn_{layer}_{rows}",
        compiler_params=pltpu.CompilerParams(dimension_semantics=("arbitrary",),
                                             vmem_limit_bytes=VMEM_LIMIT),
    )(*args)


def _sample_mix_kernel(sinks_ref, x_ref, win_ref, wout_ref, lbraw_ref, hgw_ref, g1_ref, b1_ref,
                       cr_ref, sr_ref, cs_ref, ss_ref, sret_ref, shg_ref, ck_ref, cv_ref,
                       dec_ref, inter_ref, kdec_ref, mall_ref, hblk_ref,
                       nk_all, nv_all,
                       x1_ref, nsret_ref, nshg_ref, nk_ref, nv_ref, os_scr, *, layer, gam4):
    del nk_all, nv_all
    rows, seqs = SAMPLE_ROWS, SAMPLE_SEQS
    x = x_ref[...]
    h = jnp.dot(x.astype(BF16), win_ref[...], preferred_element_type=F32)
    cr, sr, cs, ss = cr_ref[...], sr_ref[...], cs_ref[...], ss_ref[...]
    lb = _hg_lower_bound(lbraw_ref, layer)

    q = _rope(h[:, C_RQ:C_RQ + 128], cr, sr, RET_QK // 2, RET_QK)
    k = _rope(h[:, C_RK:C_RK + 128], cr, sr, RET_QK // 2, RET_QK) * (RET_QK ** -0.5)
    v = h[:, C_RV:C_RV + 256]
    o_r = _ret_intra(q, k, v, dec_ref, rows)
    hk = RET_HEADS * RET_QK
    s0 = sret_ref[...].reshape(seqs * hk, RET_V)
    r_i = (_iota((seqs * hk, 256), 0) % hk) // RET_QK
    c_i = _iota((seqs * hk, 256), 1) // RET_V
    s0bd = jnp.where(r_i == c_i, jnp.tile(s0, (1, 4)), 0.0).astype(BF16)
    own = _iota((rows, seqs * hk), 0) // DEC_SEQ == _iota((rows, seqs * hk), 1) // hk
    o_r = o_r + _mm(jnp.where(own, jnp.tile(q, (1, seqs)), 0.0), s0bd) * inter_ref[...]
    kdt = (k * kdec_ref[...]).T
    col_seq = _iota((hk, rows), 1) // DEC_SEQ
    for n in range(seqs):
        upd = _mm(jnp.where(col_seq == n, kdt, 0.0), v)
        for hd in range(RET_HEADS):
            nsret_ref[n, hd] = (gam4[hd] * sret_ref[n, hd]
                                + upd[RET_QK * hd:RET_QK * (hd + 1), RET_V * hd:RET_V * (hd + 1)])
    ret = _rms_gate(o_r, h[:, C_RG:C_RG + 256], hblk_ref)

    q = h[:, C_HQ:C_HQ + 256]
    lf = _hg_log_forget(h[:, C_HF:C_HF + 256], lb)
    k = 1.0 - jnp.exp(lf)
    v = h[:, C_HI:C_HI + 256]
    o_h, b, tail = _hgrn_intra(q, k, v, lf, mall_ref, hblk_ref, rows, SAMPLE_LEVELS)
    hk = HG_HEADS * HG_K
    s0 = shg_ref[...].reshape(seqs * hk, HG_V)
    r_i = (_iota((seqs * hk, 256), 0) % hk) // HG_K
    c_i = _iota((seqs * hk, 256), 1) // HG_V
    s0bd = jnp.where(r_i == c_i, jnp.tile(s0, (1, 4)), 0.0).astype(BF16)
    own = _iota((rows, seqs * hk), 0) // DEC_SEQ == _iota((rows, seqs * hk), 1) // hk
    o_h = o_h + _mm(jnp.where(own, jnp.tile(q * jnp.exp(b), (1, seqs)), 0.0), s0bd)
    kdt = (k * jnp.exp(tail)).T
    ebt = jnp.exp(b).T
    col_seq = _iota((hk, rows), 1) // DEC_SEQ
    for n in range(seqs):
        upd = _mm(jnp.where(col_seq == n, kdt, 0.0), v)
        last = DEC_SEQ * n + DEC_SEQ - 1
        dcol = ebt[:, last:last + 1]
        for hd in range(HG_HEADS):
            nshg_ref[n, hd] = (dcol[HG_K * hd:HG_K * (hd + 1)] * shg_ref[n, hd]
                               + upd[HG_K * hd:HG_K * (hd + 1), HG_V * hd:HG_V * (hd + 1)])
    hg = _rms_gate(o_h, h[:, C_HG:C_HG + 256], hblk_ref, hgw_ref[...])

    k = _rope(h[:, C_SK:C_SK + 128], cs, ss, ROT_DIM // 2, SWA_HD)
    v = h[:, C_SV:C_SV + 128]
    kb, vb = k.astype(BF16), v.astype(BF16)
    qm = _swa_heads(h[:, C_SQ:C_SQ + 512], lambda t: _rope(t, cs, ss, ROT_DIM // 2, SWA_HD))
    r64 = _iota((64, 1), 0)
    sinkcol = jnp.zeros((64, 1), F32)
    for hd in range(SWA_HEADS):
        sinkcol = jnp.where(r64 // 8 == hd, sinks_ref[layer, hd], sinkcol)
    rc = _iota((64, 128), 0)
    cc = _iota((64, 128), 1)
    first_seq = (rc % 8) < DEC_SEQ
    cmask = cc >= rc % DEC_SEQ
    rn = _iota((64, rows), 0)
    cn = _iota((64, rows), 1)
    new_lanes = _iota((128, 128), 1) >= WINDOW - DEC_SEQ
    pad_rows = jnp.zeros((WINDOW - 8, 128), F32)
    for pr in range(seqs // 2):
        na, nb = 2 * pr, 2 * pr + 1
        qp = jnp.concatenate([t[8 * pr:8 * pr + 8] for t in qm], axis=0)
        ka, kb_c = ck_ref[na], ck_ref[nb]
        va, vb_c = cv_ref[na], cv_ref[nb]
        sc = jnp.where(first_seq, _mm(qp, ka), _mm(qp, kb_c))
        sc = jnp.where(cmask, sc, NEG)
        nmask = (cn // DEC_SEQ == 2 * pr + (rn % 8) // DEC_SEQ) & (cn % DEC_SEQ <= rn % DEC_SEQ)
        sn = jnp.where(nmask, _mm_nt(qp, kb), NEG)
        m = jnp.maximum(jnp.maximum(jnp.max(sc, axis=1, keepdims=True),
                                    jnp.max(sn, axis=1, keepdims=True)), sinkcol)
        pc = jnp.exp(sc - m)
        pn = jnp.exp(sn - m)
        den = (jnp.sum(pc, axis=1, keepdims=True) + jnp.sum(pn, axis=1, keepdims=True)
               + jnp.exp(sinkcol - m))
        o = (_mm_nt(jnp.where(first_seq, pc, 0.0), va) + _mm_nt(jnp.where(first_seq, 0.0, pc), vb_c)
             + _mm(pn, vb)) / den
        pieces = []
        for hd in range(SWA_HEADS):
            g = hd // SWA_GROUP
            pieces.append(o[8 * hd:8 * hd + 8, SWA_HD * g:SWA_HD * (g + 1)])
        os_scr[8 * pr:8 * pr + 8, :] = jnp.concatenate(pieces, axis=1)
        for new, old_a, old_b, dst in ((k[8 * pr:8 * pr + 8], ka, kb_c, nk_ref),
                                       (v[8 * pr:8 * pr + 8], va, vb_c, nv_ref)):
            new_t = jnp.concatenate([new, pad_rows], axis=0).T
            dst[na] = jnp.where(new_lanes, pltpu.roll(new_t, WINDOW - DEC_SEQ, 1),
                                pltpu.roll(old_a, WINDOW - DEC_SEQ, 1))
            dst[nb] = jnp.where(new_lanes, pltpu.roll(new_t, WINDOW - 2 * DEC_SEQ, 1),
                                pltpu.roll(old_b, WINDOW - DEC_SEQ, 1))

    cat = jnp.concatenate([ret, hg, os_scr[...]], axis=1).astype(BF16)
    mix = jnp.dot(cat, wout_ref[...], preferred_element_type=F32)
    x1_ref[...] = _layernorm(ALPHA * x + mix, g1_ref[...], b1_ref[...])


def _sample_mix(layer, x, w_in, w_out, sinks, lbraw, hgw, g1, b1, tabs, state_ret, state_hgrn,
                cache_k, cache_v, consts, gam4, stacked):
    r, s = SAMPLE_ROWS, SAMPLE_SEQS
    wspec = lambda a, c: pl.BlockSpec((a, c), lambda i: (0, 0), pipeline_mode=pl.Buffered(1))
    vspec = lambda c: pl.BlockSpec((None, 1, c), lambda i: (layer, 0, 0))
    in_specs = [pl.BlockSpec(memory_space=pltpu.SMEM),
                pl.BlockSpec((r, D_MODEL), lambda i: (i, 0)),
                wspec(D_MODEL, IN_COLS), wspec(D_MODEL, D_MODEL),
                _const_spec((DEPTH, 256), 1), vspec(256), vspec(D_MODEL), vspec(D_MODEL)]
    in_specs += [_const_spec((r, LANES), 1)] * 4
    first_state = len(in_specs)
    in_specs += [pl.BlockSpec((None, s, RET_HEADS, RET_QK, RET_V), lambda i: (layer, i, 0, 0, 0)),
                 pl.BlockSpec((None, s, HG_HEADS, HG_K, HG_V), lambda i: (layer, i, 0, 0, 0)),
                 pl.BlockSpec((None, s, WINDOW, 128), lambda i: (layer, i, 0, 0)),
                 pl.BlockSpec((None, s, WINDOW, 128), lambda i: (layer, i, 0, 0))]
    in_specs += [_const_spec(c.shape, 1) for c in consts]
    first_stacked = len(in_specs)
    in_specs += [pl.BlockSpec(memory_space=pl.ANY)] * len(stacked)
    out_shape = [jax.ShapeDtypeStruct((DEC_BATCH * DEC_SEQ, D_MODEL), F32)]
    out_shape += [jax.ShapeDtypeStruct(a.shape, a.dtype)
                  for a in (state_ret, state_hgrn) + tuple(stacked)]
    out_specs = [pl.BlockSpec((r, D_MODEL), lambda i: (i, 0)),
                 pl.BlockSpec((None, s, RET_HEADS, RET_QK, RET_V), lambda i: (layer, i, 0, 0, 0)),
                 pl.BlockSpec((None, s, HG_HEADS, HG_K, HG_V), lambda i: (layer, i, 0, 0, 0)),
                 pl.BlockSpec((None, s, WINDOW, 128), lambda i: (layer, i, 0, 0)),
                 pl.BlockSpec((None, s, WINDOW, 128), lambda i: (layer, i, 0, 0))]
    return pl.pallas_call(
        functools.partial(_sample_mix_kernel, layer=layer, gam4=gam4),
        grid=(DEC_BATCH // s,), in_specs=in_specs, out_specs=out_specs, out_shape=out_shape,
        input_output_aliases={first_state: 1, first_state + 1: 2,
                              first_stacked: 3, first_stacked + 1: 4},
        scratch_shapes=[pltpu.VMEM((r, 512), F32)], name=f"sample_mix_{layer}",
        compiler_params=pltpu.CompilerParams(dimension_semantics=("arbitrary",),
                                             vmem_limit_bytes=VMEM_LIMIT),
    )(sinks, x, w_in, w_out, lbraw, hgw, g1, b1, *tabs, state_ret, state_hgrn, cache_k, cache_v,
      *consts, *stacked)


def _decay_levels(rows, seqlen, levels):
    t = np.arange(rows)[:, None]
    u = np.arange(rows)[None, :]
    same = (t // seqlen) == (u // seqlen)
    blocks = [same & (u <= t)]
    for l in range(levels):
        m = 1 << l
        r = (t // (2 * m)) * (2 * m) + m
        lower = t >= r
        blocks.append(np.where(lower, (u >= r) & (u <= t), (u > t) & (u <= r - 1)))
    blocks.append(same & (u > t))
    return jnp.asarray(np.concatenate(blocks, axis=0).astype(np.float32), dtype=BF16)


def _rope_tables(pos):
    pos = pos.astype(F32)[:, None]
    inv = RET_THETA ** (-jnp.linspace(0.0, 1.0, RET_QK // 2, dtype=F32))
    ang = pos * inv[None, :]
    cr = jnp.tile(jnp.concatenate([jnp.cos(ang), jnp.cos(ang)], -1), (1, RET_HEADS))
    sr = jnp.tile(jnp.concatenate([-jnp.sin(ang), jnp.sin(ang)], -1), (1, RET_HEADS))
    inv = ROPE_THETA ** (-jnp.arange(0, ROT_DIM, 2, dtype=F32) / ROT_DIM)
    ang = pos * inv[None, :]
    rest = SWA_HD - ROT_DIM
    ones = jnp.ones((ang.shape[0], rest), F32)
    zeros = jnp.zeros((ang.shape[0], rest), F32)
    cs = jnp.tile(jnp.concatenate([jnp.cos(ang), jnp.cos(ang), ones], -1), (1, 2))
    ss = jnp.tile(jnp.concatenate([-jnp.sin(ang), jnp.sin(ang), zeros], -1), (1, 2))
    return cr, sr, cs, ss


def _ret_tables(tok, same):
    lg = jnp.log(1.0 - 2.0 ** (-5.0 - jnp.arange(RET_HEADS, dtype=F32)))
    n_tok = int(tok.max()) + 1
    tokf = jnp.asarray(tok, F32)
    diff = tokf[:, None] - tokf[None, :]
    dec = jnp.where(jnp.asarray(same) & (diff >= 0),
                    jnp.exp(jnp.maximum(diff, 0.0)[None] * lg[:, None, None]), 0.0)
    inter = jnp.repeat(jnp.exp((tokf + 1.0)[:, None] * lg[None, :]), RET_V, axis=1)
    kdec = jnp.repeat(jnp.exp((n_tok - 1.0 - tokf)[:, None] * lg[None, :]), RET_QK, axis=1)
    total = jnp.exp(n_tok * lg)
    return dec, inter, kdec, total


def kernel(x_prompt, x_sample, state_ret, state_hgrn, cache_swa_k, cache_swa_v, w_in, w_out,
           swa_sinks, hg_lb_raw, hg_norm_w, ln_mix_g, ln_mix_b, w_up, w_down, ln_ffn_g, ln_ffn_b):
    w_in_b, w_out_b = w_in[0].astype(BF16), w_out[0].astype(BF16)
    hgw = jnp.tile(hg_norm_w, (1, HG_HEADS)).reshape(DEPTH, 1, HG_HEADS * HG_V)
    g1, b1 = ln_mix_g.reshape(DEPTH, 1, D_MODEL), ln_mix_b.reshape(DEPTH, 1, D_MODEL)
    g2, b2 = ln_ffn_g.reshape(DEPTH, 1, D_MODEL), ln_ffn_b.reshape(DEPTH, 1, D_MODEL)
    cache_k = cache_swa_k.transpose(0, 1, 3, 4, 2).reshape(DEPTH, DEC_BATCH, 128, WINDOW)
    cache_v = cache_swa_v.transpose(0, 1, 3, 4, 2).reshape(DEPTH, DEC_BATCH, 128, WINDOW)

    hblk = jnp.asarray((np.arange(256)[:, None] // 64 == np.arange(256)[None, :] // 64)
                       .astype(np.float32), dtype=BF16)
    p_tabs = _rope_tables(jnp.arange(SEQ))
    tok = np.arange(CHUNK)
    dec, inter, kdec, total = _ret_tables(tok, np.ones((CHUNK, CHUNK), bool))
    rowdec = jnp.broadcast_to(jnp.repeat(total, RET_QK)[:, None], (128, 256))
    t_i, s_i = np.arange(CHUNK)[:, None], np.arange(CHUNK)[None, :]
    tri = jnp.asarray((s_i <= t_i).astype(np.float32), dtype=BF16)
    lvl = np.where(s_i < t_i, np.floor(np.log2(np.maximum(t_i ^ s_i, 1))),
                   np.where(s_i == t_i, PROMPT_LEVELS, -1)).astype(np.int32)
    q_i, k_j = np.arange(CHUNK)[:, None], np.arange(2 * CHUNK)[None, :]
    band = np.where((k_j >= q_i) & (k_j <= q_i + WINDOW), np.where(k_j >= WINDOW, 2, 1), 0)
    p_consts = (dec, inter, kdec, rowdec, tri, hblk, jnp.asarray(lvl),
                jnp.asarray(band.astype(np.int32)))
    row = np.arange(SAMPLE_ROWS)
    s_tabs = _rope_tables(PAST_LEN + jnp.asarray(row % DEC_SEQ))
    same = (row[:, None] // DEC_SEQ) == (row[None, :] // DEC_SEQ)
    dec_s, inter_s, kdec_s, total_s = _ret_tables(row % DEC_SEQ, same)
    s_consts = (dec_s, inter_s, kdec_s, _decay_levels(SAMPLE_ROWS, DEC_SEQ, SAMPLE_LEVELS), hblk)
    gam4 = tuple(float((1.0 - 2.0 ** (-5.0 - h)) ** DEC_SEQ) for h in range(RET_HEADS))
    del total_s

    xp = x_prompt.reshape(BATCH * SEQ, D_MODEL)
    xs = x_sample.reshape(DEC_BATCH * DEC_SEQ, D_MODEL)
    outs = [[] for _ in range(4)]
    ns_r, ns_h = state_ret, state_hgrn
    stacked = (jnp.zeros((DEPTH, DEC_BATCH, 128, WINDOW), F32),
               jnp.zeros((DEPTH, DEC_BATCH, 128, WINDOW), F32))
    for l in range(DEPTH):
        x1, st_r, st_h, pk, pv, w_up_b, w_down_b = _prompt_mix(
            l, xp, w_in_b, w_out_b, swa_sinks, hg_lb_raw, hgw, g1, b1, p_tabs, p_consts,
            w_up, w_down)
        ffn_out = _ffn(l, x1, w_up_b, w_down_b, g2, b2,
                       (w_in, w_out) if l + 1 < DEPTH else None)
        xp = ffn_out[0]
        outs[0].append(jnp.stack([st_r[:, RET_QK * h:RET_QK * (h + 1), RET_V * h:RET_V * (h + 1)]
                                  for h in range(RET_HEADS)], axis=1))
        outs[1].append(jnp.stack([st_h[:, HG_K * h:HG_K * (h + 1), HG_V * h:HG_V * (h + 1)]
                                  for h in range(HG_HEADS)], axis=1))
        outs[2].append(pk.reshape(BATCH, SWA_KV, SWA_HD, WINDOW).transpose(0, 3, 1, 2))
        outs[3].append(pv.reshape(BATCH, SWA_KV, SWA_HD, WINDOW).transpose(0, 3, 1, 2))

        x1, ns_r, ns_h, *stacked = _sample_mix(l, xs, w_in_b, w_out_b, swa_sinks, hg_lb_raw, hgw,
                                               g1, b1, s_tabs, ns_r, ns_h, cache_k, cache_v,
                                               s_consts, gam4, tuple(stacked))
        xs = _ffn(l, x1, w_up_b, w_down_b, g2, b2)[0]
        if l + 1 < DEPTH:
            w_in_b, w_out_b = ffn_out[1], ffn_out[2]

    nk, nv = stacked
    nk = nk.reshape(DEPTH, DEC_BATCH, SWA_KV, SWA_HD, WINDOW).transpose(0, 1, 4, 2, 3)
    nv = nv.reshape(DEPTH, DEC_BATCH, SWA_KV, SWA_HD, WINDOW).transpose(0, 1, 4, 2, 3)
    return ((xp.reshape(BATCH, SEQ, D_MODEL), xs.reshape(DEC_BATCH, DEC_SEQ, D_MODEL))
            + tuple(jnp.stack(o) for o in outs) + (ns_r, ns_h, nk, nv))
```
